```python
import math
import jax, jax.numpy as jnp
from jax import lax
import numpy as np

D_MODEL = 1024
BATCH = 4
SEQ = 4096
DEPTH = 1
DEC_BATCH = 4
DEC_SEQ = 8192
PAST_LEN = 128

HEAD_DIM = 64
N_Q_HEADS = 8
N_KV_HEADS = 2
GROUP = N_Q_HEADS // N_KV_HEADS
ATTN_W = N_Q_HEADS * HEAD_DIM
KV_W = N_KV_HEADS * HEAD_DIM
POOL_WINDOWS = (2, 4, 8, 16)
POOL_CH = 128
POOL_W = POOL_CH * len(POOL_WINDOWS)
MIX_W = ATTN_W + POOL_W
IN_W = ATTN_W + 2 * KV_W + POOL_W
D_FF = 4 * D_MODEL
PLE_DIM = 256
GRID_W = 64
ROPE_THETA = 10000.0
ROPE_PAIRS = HEAD_DIM // 4
Q_BLOCK = 128
EPS = 1e-6

kernel_name = "hybrid_gqa_pool_encoder"


def _rmsnorm(x, g):
    xf = x.astype(jnp.float32)
    y = xf * lax.rsqrt(jnp.mean(xf * xf, axis=-1, keepdims=True) + EPS)
    return (y * g.astype(jnp.float32)).astype(x.dtype)


def _grid_angles(T):
    rows = T // GRID_W
    row = jnp.repeat(jnp.arange(rows, dtype=jnp.float32), GRID_W)
    col = jnp.tile(jnp.arange(GRID_W, dtype=jnp.float32), rows)
    inv_freq = ROPE_THETA ** (-jnp.arange(ROPE_PAIRS, dtype=jnp.float32) / ROPE_PAIRS)
    return row[:, None] * inv_freq[None, :], col[:, None] * inv_freq[None, :]


def _rope_half(x, ang):
    x1, x2 = x[..., :ROPE_PAIRS], x[..., ROPE_PAIRS:]
    c = jnp.cos(ang)[None, :, None, :]
    s = jnp.sin(ang)[None, :, None, :]
    return jnp.concatenate([x1 * c - x2 * s, x2 * c + x1 * s], axis=-1)


def _axial_rope(x, ang_row, ang_col):
    xf = x.astype(jnp.float32)
    half = HEAD_DIM // 2
    y = jnp.concatenate([_rope_half(xf[..., :half], ang_row),
                         _rope_half(xf[..., half:], ang_col)], axis=-1)
    return y.astype(x.dtype)


def _attention(q, k, v, q_g, k_g):
    B, T = q.shape[0], q.shape[1]
    ang_row, ang_col = _grid_angles(T)
    q = _axial_rope(_rmsnorm(q, q_g), ang_row, ang_col)
    k = _axial_rope(_rmsnorm(k, k_g), ang_row, ang_col)
    scale = 1.0 / math.sqrt(HEAD_DIM)
    nb = T // Q_BLOCK
    qb = q.reshape(B, nb, Q_BLOCK, N_KV_HEADS, GROUP, HEAD_DIM).transpose(1, 0, 2, 3, 4, 5)

    def block(qi):
        s = jnp.einsum('bqkgd,bskd->bkgqs', qi, k).astype(jnp.float32) * scale
        p = jax.nn.softmax(s, axis=-1).astype(v.dtype)
        return jnp.einsum('bkgqs,bskd->bqkgd', p, v)

    o = lax.map(block, qb)
    return o.transpose(1, 0, 2, 3, 4, 5).reshape(B, T, ATTN_W)


def _pool_mixer(u, w_pool, pool_scale):
    B, T, C = u.shape
    uf = u.astype(jnp.float32)
    cs = jnp.concatenate([jnp.zeros((B, 1, C), jnp.float32), jnp.cumsum(uf, axis=1)], axis=1)
    t = jnp.arange(T)
    outs = []
    for g, w in enumerate(POOL_WINDOWS):
        half = w // 2
        lo = jnp.clip(t - half, 0, T)
        hi = jnp.clip(t + half, 0, T)
        csg = cs[:, :, g * POOL_CH:(g + 1) * POOL_CH]
        win_sum = jnp.take(csg, hi, axis=1) - jnp.take(csg, lo, axis=1)
        mean = win_sum / (hi - lo).astype(jnp.float32)[None, :, None]
        outs.append(mean - uf[:, :, g * POOL_CH:(g + 1) * POOL_CH])
    d = jnp.stack(outs, axis=2).astype(u.dtype)
    y = jnp.einsum('btgc,gcd->btgd', d, w_pool).reshape(B, T, C)
    return y * pool_scale


def _layer(h, p_i, norm_mix_g, w_in, q_norm_g, k_norm_g, w_pool, pool_scale, w_out,
           norm_mlp_g, w_up, w_down, norm_ple_g, w_ple_gate, w_ple_proj):
    B, T, _ = h.shape
    a = _rmsnorm(h, norm_mix_g)
    z = a @ w_in
    q = z[..., :ATTN_W].reshape(B, T, N_Q_HEADS, HEAD_DIM)
    k = z[..., ATTN_W:ATTN_W + KV_W].reshape(B, T, N_KV_HEADS, HEAD_DIM)
    v = z[..., ATTN_W + KV_W:ATTN_W + 2 * KV_W].reshape(B, T, N_KV_HEADS, HEAD_DIM)
    u = z[..., ATTN_W + 2 * KV_W:]
    o_attn = _attention(q, k, v, q_norm_g, k_norm_g)
    o_pool = _pool_mixer(u, w_pool, pool_scale)
    h = h + jnp.concatenate([o_attn, o_pool], axis=-1) @ w_out
    m = _rmsnorm(h, norm_mlp_g)
    h = h + jnp.square(jax.nn.relu(m @ w_up)) @ w_down
    gate = jax.nn.sigmoid((_rmsnorm(h, norm_ple_g) @ w_ple_gate).astype(jnp.float32)).astype(h.dtype)
    h = h + gate * (p_i @ w_ple_proj)
    return h


def _trunk(x, p, norm_mix_g, w_in, q_norm_g, k_norm_g, w_pool, pool_scale, w_out,
           norm_mlp_g, w_up, w_down, norm_ple_g, w_ple_gate, w_ple_proj, final_norm_g):
    h = x
    for i in range(DEPTH):
        h = _layer(h, p[i], norm_mix_g[i], w_in[i], q_norm_g[i], k_norm_g[i], w_pool[i],
                   pool_scale[i], w_out[i], norm_mlp_g[i], w_up[i], w_down[i],
                   norm_ple_g[i], w_ple_gate[i], w_ple_proj[i])
    return _rmsnorm(h, final_norm_g)


def setup_inputs(seed: int = 0) -> dict:
    key = jax.random.key(seed)
    ks = jax.random.split(key, 20)
    f32 = jnp.float32

    def nrm(k, shape, scale):
        return jax.random.normal(k, shape, f32) * scale

    def gain(k, shape):
        return 1.0 + 0.05 * jax.random.normal(k, shape, f32)

    return {
        "x_prompt": nrm(ks[0], (BATCH, SEQ, D_MODEL), 1.0),
        "x_sample": nrm(ks[1], (DEC_BATCH, DEC_SEQ, D_MODEL), 1.0),
        "p_prompt": nrm(ks[2], (DEPTH, BATCH, SEQ, PLE_DIM), 1.0),
        "p_sample": nrm(ks[3], (DEPTH, DEC_BATCH, DEC_SEQ, PLE_DIM), 1.0),
        "norm_mix_g": gain(ks[4], (DEPTH, D_MODEL)),
        "w_in": nrm(ks[5], (DEPTH, D_MODEL, IN_W), D_MODEL ** -0.5),
        "q_norm_g": gain(ks[6], (DEPTH, HEAD_DIM)),
        "k_norm_g": gain(ks[7], (DEPTH, HEAD_DIM)),
        "w_pool": nrm(ks[8], (DEPTH, len(POOL_WINDOWS), POOL_CH, POOL_CH), POOL_CH ** -0.5),
        "pool_scale": gain(ks[9], (DEPTH, POOL_W)),
        "w_out": nrm(ks[10], (DEPTH, MIX_W, D_MODEL), MIX_W ** -0.5),
        "norm_mlp_g": gain(ks[11], (DEPTH, D_MODEL)),
        "w_up": nrm(ks[12], (DEPTH, D_MODEL, D_FF), D_MODEL ** -0.5),
        "w_down": nrm(ks[13], (DEPTH, D_FF, D_MODEL), D_FF ** -0.5),
        "norm_ple_g": gain(ks[14], (DEPTH, D_MODEL)),
        "w_ple_gate": nrm(ks[15], (DEPTH, D_MODEL, D_MODEL), D_MODEL ** -0.5),
        "w_ple_proj": nrm(ks[16], (DEPTH, PLE_DIM, D_MODEL), PLE_DIM ** -0.5),
        "final_norm_g": gain(ks[17], (D_MODEL,)),
    }


def reference(x_prompt, x_sample, p_prompt, p_sample, norm_mix_g, w_in, q_norm_g, k_norm_g,
              w_pool, pool_scale, w_out, norm_mlp_g, w_up, w_down, norm_ple_g, w_ple_gate,
              w_ple_proj, final_norm_g):
    y_prompt = _trunk(x_prompt, p_prompt, norm_mix_g, w_in, q_norm_g, k_norm_g, w_pool, pool_scale,
                      w_out, norm_mlp_g, w_up, w_down, norm_ple_g, w_ple_gate, w_ple_proj, final_norm_g)
    y_sample = _trunk(x_sample, p_sample, norm_mix_g, w_in, q_norm_g, k_norm_g, w_pool, pool_scale,
                      w_out, norm_mlp_g, w_up, w_down, norm_ple_g, w_ple_gate, w_ple_proj, final_norm_g)
    return (y_prompt, y_sample)
```

```python
import functools
import math

import jax
import jax.numpy as jnp
from jax import lax
from jax.experimental import pallas as pl
from jax.experimental.pallas import tpu as pltpu

F32 = jnp.float32
BF16 = jnp.bfloat16

HEAD_DIM = 64
N_Q_HEADS = 8
N_KV_HEADS = 2
ATTN_W = N_Q_HEADS * HEAD_DIM
KV_W = N_KV_HEADS * HEAD_DIM
POOL_WINDOWS = (2, 4, 8, 16)
POOL_CH = 128
POOL_W = POOL_CH * len(POOL_WINDOWS)
GRID_W = 64
ROPE_THETA = 10000.0
ROPE_PAIRS = HEAD_DIM // 4
EPS = 1e-6

LANES = 128
SUBLANES = 8
POOL_HALO = max(POOL_WINDOWS) // 2
assert POOL_HALO == SUBLANES

VMEM_LIMIT_BYTES = 56 * 1024 * 1024


def _tiles(T):
    tm = min(512, T)
    assert T % tm == 0 and tm % GRID_W == 0
    return tm, tm, min(512, T), tm


def _rms(x, g):
    ms = jnp.mean(x * x, axis=-1, keepdims=True)
    return x * lax.rsqrt(ms + EPS) * g


def _proj_body(x_ref, gmix_ref, win_ref, gq_ref, gk_ref, invf_ref, bd_ref,
               q_ref, k_ref, v_ref, u_ref,
               rowcos_ref, rowsin_ref, colcos_ref, colsin_ref, *, tm, seq):
    b = pl.program_id(0)
    i = pl.program_id(1)
    lane = lax.broadcasted_iota(jnp.int32, (1, LANES), 1)
    is_row_lane = (lane & 32) == 0
    first_half = (lane & 16) == 0
    low_head = lane < HEAD_DIM

    @pl.when((b == 0) & (i == 0))
    def _init_tables():
        invf = invf_ref[...]
        sign = jnp.where(first_half, -1.0, 1.0).astype(F32)
        rpos = lax.broadcasted_iota(jnp.int32, (seq // GRID_W, LANES), 0).astype(F32)
        rang = rpos * invf
        rowcos_ref[...] = jnp.cos(rang)
        rowsin_ref[...] = jnp.sin(rang) * sign
        cpos = lax.broadcasted_iota(jnp.int32, (GRID_W, LANES), 0).astype(F32)
        cang = cpos * invf
        colcos_ref[...] = jnp.cos(cang)
        colsin_ref[...] = jnp.sin(cang) * sign

    x = x_ref[0]
    a = _rms(x, gmix_ref[...]).astype(BF16)
    z = jnp.dot(a, win_ref[...], preferred_element_type=F32)

    colcos = colcos_ref[...]
    colsin = colsin_ref[...]
    cparts, sparts = [], []
    for rg in range(tm // GRID_W):
        r = i * (tm // GRID_W) + rg
        cparts.append(jnp.where(is_row_lane, rowcos_ref[pl.ds(r, 1), :], colcos))
        sparts.append(jnp.where(is_row_lane, rowsin_ref[pl.ds(r, 1), :], colsin))
    cos_t = jnp.concatenate(cparts, axis=0)
    sin_t = jnp.concatenate(sparts, axis=0)

    def rope(xs):
        swapped = jnp.where(first_half, pltpu.roll(xs, LANES - ROPE_PAIRS, 1), pltpu.roll(xs, ROPE_PAIRS, 1))
        return xs * cos_t + swapped * sin_t

    def head_mean_sq(t, bd):
        sq = t * t
        hi = sq.astype(BF16)
        lo = (sq - hi.astype(F32)).astype(BF16)
        s = jnp.dot(hi, bd, preferred_element_type=F32) + jnp.dot(lo, bd, preferred_element_type=F32)
        return s * (1.0 / HEAD_DIM)

    qz = z[:, :ATTN_W]
    q_ms = head_mean_sq(qz, bd_ref[...])
    qn = qz * lax.rsqrt(q_ms + EPS) * gq_ref[...]
    scale = 1.0 / math.sqrt(HEAD_DIM)
    for s in range(ATTN_W // LANES):
        sl = slice(s * LANES, (s + 1) * LANES)
        q_ref[0, :, sl] = (rope(qn[:, sl]) * scale).astype(BF16)

    kz = z[:, ATTN_W:ATTN_W + KV_W]
    k_ms = head_mean_sq(kz, bd_ref[:KV_W, :KV_W])
    kr = rope(kz * lax.rsqrt(k_ms + EPS) * gk_ref[...])
    kr_sw = pltpu.roll(kr, HEAD_DIM, 1)
    vz = z[:, ATTN_W + KV_W:ATTN_W + 2 * KV_W]
    vz_sw = pltpu.roll(vz, HEAD_DIM, 1)
    zero = jnp.zeros_like(kr)
    ones_lo = jnp.broadcast_to(jnp.where(low_head, 1.0, 0.0).astype(BF16), (tm, LANES))
    ones_hi = jnp.broadcast_to(jnp.where(low_head, 0.0, 1.0).astype(BF16), (tm, LANES))
    for g, (k_lo, k_hi, v_lo, v_hi) in enumerate(((kr, kr_sw, vz, vz_sw), (kr_sw, kr, vz_sw, vz))):
        k_ref[0, g, 0] = jnp.where(low_head, k_lo, zero).astype(BF16)
        k_ref[0, g, 1] = jnp.where(low_head, zero, k_hi).astype(BF16)
        v_ref[0, g, 0, :, :LANES] = jnp.where(low_head, v_lo, zero).astype(BF16)
        v_ref[0, g, 0, :, LANES:] = ones_lo
        v_ref[0, g, 1, :, :LANES] = jnp.where(low_head, zero, v_hi).astype(BF16)
        v_ref[0, g, 1, :, LANES:] = ones_hi

    u_ref[0] = z[:, ATTN_W + 2 * KV_W:]


def _proj_call(x, gmix, w_in, gq, gk, invf, bd):
    B, T, D = x.shape
    tm = _tiles(T)[0]
    in_w = w_in.shape[1]
    const = lambda b, i: (0, 0)
    return pl.pallas_call(
        functools.partial(_proj_body, tm=tm, seq=T),
        grid=(B, T // tm),
        in_specs=[
            pl.BlockSpec((1, tm, D), lambda b, i: (b, i, 0)),
            pl.BlockSpec((1, D), const),
            pl.BlockSpec((D, in_w), const),
            pl.BlockSpec((1, ATTN_W), const),
            pl.BlockSpec((1, KV_W), const),
            pl.BlockSpec((1, LANES), const),
            pl.BlockSpec((ATTN_W, ATTN_W), const),
        ],
        out_specs=[
            pl.BlockSpec((1, tm, ATTN_W), lambda b, i: (b, i, 0)),
            pl.BlockSpec((1, N_KV_HEADS, 2, tm, LANES), lambda b, i: (b, 0, 0, i, 0)),
            pl.BlockSpec((1, N_KV_HEADS, 2, tm, 2 * LANES), lambda b, i: (b, 0, 0, i, 0)),
            pl.BlockSpec((1, tm, POOL_W), lambda b, i: (b, i, 0)),
        ],
        out_shape=[
            jax.ShapeDtypeStruct((B, T, ATTN_W), BF16),
            jax.ShapeDtypeStruct((B, N_KV_HEADS, 2, T, LANES), BF16),
            jax.ShapeDtypeStruct((B, N_KV_HEADS, 2, T, 2 * LANES), BF16),
            jax.ShapeDtypeStruct((B, T, POOL_W), F32),
        ],
        scratch_shapes=[
            pltpu.VMEM((T // GRID_W, LANES), F32),
            pltpu.VMEM((T // GRID_W, LANES), F32),
            pltpu.VMEM((GRID_W, LANES), F32),
            pltpu.VMEM((GRID_W, LANES), F32),
        ],
        compiler_params=pltpu.CompilerParams(
            dimension_semantics=("arbitrary", "arbitrary"), vmem_limit_bytes=VMEM_LIMIT_BYTES),
        name="proj",
    )(x, gmix, w_in, gq, gk, invf, bd)


def _attn_body(q_ref, k_ref, v_ref, o_ref, acc_ref, m_ref, *, tq, tk, seq):
    lane2 = lax.broadcasted_iota(jnp.int32, (1, 2 * LANES), 1)
    first_head_lane = (lane2 & HEAD_DIM) == 0
    contract_last = (((1,), (1,)), ((), ()))

    for pair in range(2):
        qp = q_ref[0, :, pair * LANES:(pair + 1) * LANES]
        acc_ref[...] = jnp.zeros_like(acc_ref)
        m_ref[...] = jnp.full_like(m_ref, -jnp.inf)

        def step(j, carry):
            rows = pl.ds(pl.multiple_of(j * tk, tk), tk)
            s_a = lax.dot_general(qp, k_ref[0, 0, 0, rows, :], contract_last, preferred_element_type=F32)
            s_b = lax.dot_general(qp, k_ref[0, 0, 1, rows, :], contract_last, preferred_element_type=F32)
            m_a, m_b = m_ref[0], m_ref[1]
            n_a = jnp.maximum(m_a, jnp.max(s_a, axis=1, keepdims=True))
            n_b = jnp.maximum(m_b, jnp.max(s_b, axis=1, keepdims=True))
            p_a = jnp.exp(s_a - n_a).astype(BF16)
            p_b = jnp.exp(s_b - n_b).astype(BF16)
            alpha = jnp.where(first_head_lane, jnp.exp(m_a - n_a), jnp.exp(m_b - n_b))
            pv = (jnp.dot(p_a, v_ref[0, 0, 0, rows, :], preferred_element_type=F32)
                  + jnp.dot(p_b, v_ref[0, 0, 1, rows, :], preferred_element_type=F32))
            acc_ref[...] = alpha * acc_ref[...] + pv
            m_ref[0] = n_a
            m_ref[1] = n_b
            return carry

        lax.fori_loop(0, seq // tk, step, 0)
        acc = acc_ref[...]
        o_ref[0, :, pair * LANES:(pair + 1) * LANES] = (acc[:, :LANES] / acc[:, LANES:]).astype(BF16)


def _attn_call(q, k, v):
    B, T, _ = q.shape
    _, tq, tk, _ = _tiles(T)
    return pl.pallas_call(
        functools.partial(_attn_body, tq=tq, tk=tk, seq=T),
        grid=(B, N_KV_HEADS, T // tq),
        in_specs=[
            pl.BlockSpec((1, tq, 2 * LANES), lambda b, g, i: (b, i, g)),
            pl.BlockSpec((1, 1, 2, T, LANES), lambda b, g, i: (b, g, 0, 0, 0)),
            pl.BlockSpec((1, 1, 2, T, 2 * LANES), lambda b, g, i: (b, g, 0, 0, 0)),
        ],
        out_specs=pl.BlockSpec((1, tq, 2 * LANES), lambda b, g, i: (b, i, g)),
        out_shape=jax.ShapeDtypeStruct((B, T, ATTN_W), BF16),
        scratch_shapes=[
            pltpu.VMEM((tq, 2 * LANES), F32),
            pltpu.VMEM((2, tq, 1), F32),
        ],
        compiler_params=pltpu.CompilerParams(
            dimension_semantics=("arbitrary", "arbitrary", "arbitrary"), vmem_limit_bytes=VMEM_LIMIT_BYTES),
        name="attn",
    )(q, k, v)


def _mix_body(x_ref, o_ref, u_ref, uprev_ref, unext_ref, p_ref,
              wpool_ref, pscale_ref, wout_ref, gmlp_ref, wup_ref, wdown_ref,
              gple_ref, wgate_ref, wproj_ref, gfin_ref,
              y_ref, ext_ref, *, tm, seq, ff_chunk, final_norm):
    i = pl.program_id(1)
    n_tiles = pl.num_programs(1)

    ext_ref[0:POOL_HALO, :] = jnp.where(i > 0, uprev_ref[0], 0.0)
    ext_ref[POOL_HALO:POOL_HALO + tm, :] = u_ref[0]
    ext_ref[POOL_HALO + tm:, :] = jnp.where(i < n_tiles - 1, unext_ref[0], 0.0)

    t = i * tm + lax.broadcasted_iota(jnp.int32, (tm, 1), 0)
    mix = jnp.dot(o_ref[0], wout_ref[:ATTN_W, :], preferred_element_type=F32)
    for g, w in enumerate(POOL_WINDOWS):
        half = w // 2
        sl = slice(g * POOL_CH, (g + 1) * POOL_CH)
        wsum = ext_ref[POOL_HALO - half:POOL_HALO - half + tm, sl]
        for off in range(-half + 1, half):
            wsum = wsum + ext_ref[POOL_HALO + off:POOL_HALO + off + tm, sl]
        cnt = (jnp.minimum(t + half, seq) - jnp.maximum(t - half, 0)).astype(F32)
        d = (wsum / cnt - ext_ref[POOL_HALO:POOL_HALO + tm, sl]).astype(BF16)
        yg = jnp.dot(d, wpool_ref[g], preferred_element_type=F32) * pscale_ref[:, sl]
        mix = mix + jnp.dot(yg.astype(BF16), wout_ref[ATTN_W + g * POOL_CH:ATTN_W + (g + 1) * POOL_CH, :],
                            preferred_element_type=F32)
    h = x_ref[0] + mix

    m = _rms(h, gmlp_ref[...]).astype(BF16)
    d_ff = wup_ref.shape[1]
    mlp = None
    for c in range(d_ff // ff_chunk):
        cs = slice(c * ff_chunk, (c + 1) * ff_chunk)
        up = jnp.dot(m, wup_ref[:, cs], preferred_element_type=F32)
        act = jnp.square(jnp.maximum(up, 0.0)).astype(BF16)
        part = jnp.dot(act, wdown_ref[cs, :], preferred_element_type=F32)
        mlp = part if mlp is None else mlp + part
    h = h + mlp

    gate_in = _rms(h, gple_ref[...]).astype(BF16)
    gate = jax.nn.sigmoid(jnp.dot(gate_in, wgate_ref[...], preferred_element_type=F32))
    emb = jnp.dot(p_ref[0].astype(BF16), wproj_ref[...], preferred_element_type=F32)
    h = h + gate * emb

    y_ref[0] = _rms(h, gfin_ref[...]) if final_norm else h


def _mix_call(x, o, u, p, wpool, pscale, wout, gmlp, wup, wdown, gple, wgate, wproj, gfin, final_norm):
    B, T, D = x.shape
    tm = _tiles(T)[3]
    d_ff = wup.shape[1]
    ple = p.shape[-1]
    halo_blocks = tm // POOL_HALO
    n_halo = T // POOL_HALO

    def const(ndim):
        return lambda b, i: (0,) * ndim

    def resident(shape):
        return pl.BlockSpec(shape, const(len(shape)), pipeline_mode=pl.Buffered(1))

    tile = lambda w: pl.BlockSpec((1, tm, w), lambda b, i: (b, i, 0))
    return pl.pallas_call(
        functools.partial(_mix_body, tm=tm, seq=T, ff_chunk=min(1024, d_ff), final_norm=final_norm),
        grid=(B, T // tm),
        in_specs=[
            tile(D),
            tile(ATTN_W),
            tile(POOL_W),
            pl.BlockSpec((1, POOL_HALO, POOL_W), lambda b, i: (b, jnp.maximum(i * halo_blocks - 1, 0), 0)),
            pl.BlockSpec((1, POOL_HALO, POOL_W), lambda b, i: (b, jnp.minimum((i + 1) * halo_blocks, n_halo - 1), 0)),
            tile(ple),
            resident((len(POOL_WINDOWS), POOL_CH, POOL_CH)),
            resident((1, POOL_W)),
            resident((ATTN_W + POOL_W, D)),
            resident((1, D)),
            resident((D, d_ff)),
            resident((d_ff, D)),
            resident((1, D)),
            resident((D, D)),
            resident((ple, D)),
            resident((1, D)),
        ],
        out_specs=tile(D),
        out_shape=jax.ShapeDtypeStruct((B, T, D), F32),
        scratch_shapes=[pltpu.VMEM((tm + 2 * POOL_HALO, POOL_W), F32)],
        compiler_params=pltpu.CompilerParams(
            dimension_semantics=("arbitrary", "arbitrary"), vmem_limit_bytes=VMEM_LIMIT_BYTES),
        name="mix",
    )(x, o, u, u, u, p, wpool, pscale, wout, gmlp, wup, wdown, gple, wgate, wproj, gfin)


def _trunk(x, p, consts, layers, final_g):
    invf, bd = consts
    h = x
    for li, lw in enumerate(layers):
        q, k, v, u = _proj_call(h, lw["gmix"], lw["w_in"], lw["gq"], lw["gk"], invf, bd)
        o = _attn_call(q, k, v)
        h = _mix_call(h, o, u, p[li], lw["wpool"], lw["pscale"], lw["wout"], lw["gmlp"], lw["wup"], lw["wdown"],
                      lw["gple"], lw["wgate"], lw["wproj"], final_g, final_norm=(li == len(layers) - 1))
    return h


def kernel(x_prompt, x_sample, p_prompt, p_sample, norm_mix_g, w_in, q_norm_g, k_norm_g, w_pool, pool_scale, w_out, norm_mlp_g, w_up, w_down, norm_ple_g, w_ple_gate, w_ple_proj, final_norm_g):
    depth = w_in.shape[0]
    row = lambda g: g.reshape(1, -1).astype(F32)
    layers = []
    for li in range(depth):
        layers.append(dict(
            gmix=row(norm_mix_g[li]), w_in=w_in[li].astype(BF16),
            gq=row(jnp.tile(q_norm_g[li], N_Q_HEADS)), gk=row(jnp.tile(k_norm_g[li], N_KV_HEADS)),
            wpool=w_pool[li].astype(BF16), pscale=row(pool_scale[li]), wout=w_out[li].astype(BF16),
            gmlp=row(norm_mlp_g[li]), wup=w_up[li].astype(BF16), wdown=w_down[li].astype(BF16),
            gple=row(norm_ple_g[li]), wgate=w_ple_gate[li].astype(BF16), wproj=w_ple_proj[li].astype(BF16)))
    inv_freq = ROPE_THETA ** (-jnp.arange(ROPE_PAIRS, dtype=F32) / ROPE_PAIRS)
    invf = jnp.tile(inv_freq, LANES // ROPE_PAIRS).reshape(1, LANES)
    head_of = jnp.arange(ATTN_W) // HEAD_DIM
    bd = (head_of[:, None] == head_of[None, :]).astype(BF16)
    consts = (invf, bd)
    final_g = row(final_norm_g)
    y_prompt = _trunk(x_prompt, p_prompt, consts, layers, final_g)
    y_sample = _trunk(x_sample, p_sample, consts, layers, final_g)
    return (y_prompt, y_sample)
```

```python
import functools
import math

import jax
import jax.numpy as jnp
from jax import lax
from jax.experimental import pallas as pl
from jax.experimental.pallas import tpu as pltpu

F32 = jnp.float32
BF16 = jnp.bfloat16

HEAD_DIM = 64
N_Q_HEADS = 8
N_KV_HEADS = 2
ATTN_W = N_Q_HEADS * HEAD_DIM
KV_W = N_KV_HEADS * HEAD_DIM
POOL_WINDOWS = (2, 4, 8, 16)
POOL_CH = 128
POOL_W = POOL_CH * len(POOL_WINDOWS)
GRID_W = 64
ROPE_THETA = 10000.0
ROPE_PAIRS = HEAD_DIM // 4
EPS = 1e-6

LANES = 128
SUBLANES = 8
POOL_HALO = max(POOL_WINDOWS) // 2
assert POOL_HALO == SUBLANES

VMEM_LIMIT_BYTES = 56 * 1024 * 1024


def _tiles(T):
    tm = min(512, T)
    assert T % tm == 0 and tm % GRID_W == 0
    return tm, tm, min(512, T), tm


def _rms(x, g):
    ms = jnp.mean(x * x, axis=-1, keepdims=True)
    return x * lax.rsqrt(ms + EPS) * g


def _proj_body(x_ref, gmix_ref, win_ref, gq_ref, gk_ref, invf_ref, bd_ref,
               q_ref, k_ref, v_ref, u_ref,
               rowcos_ref, rowsin_ref, colcos_ref, colsin_ref, *, tm, seq):
    b = pl.program_id(0)
    i = pl.program_id(1)
    lane = lax.broadcasted_iota(jnp.int32, (1, LANES), 1)
    is_row_lane = (lane & 32) == 0
    first_half = (lane & 16) == 0
    low_head = lane < HEAD_DIM

    @pl.when((b == 0) & (i == 0))
    def _init_tables():
        invf = invf_ref[...]
        sign = jnp.where(first_half, -1.0, 1.0).astype(F32)
        rpos = lax.broadcasted_iota(jnp.int32, (seq // GRID_W, LANES), 0).astype(F32)
        rang = rpos * invf
        rowcos_ref[...] = jnp.cos(rang)
        rowsin_ref[...] = jnp.sin(rang) * sign
        cpos = lax.broadcasted_iota(jnp.int32, (GRID_W, LANES), 0).astype(F32)
        cang = cpos * invf
        colcos_ref[...] = jnp.cos(cang)
        colsin_ref[...] = jnp.sin(cang) * sign

    x = x_ref[0]
    a = _rms(x, gmix_ref[...]).astype(BF16)
    z = jnp.dot(a, win_ref[...], preferred_element_type=F32)

    colcos = colcos_ref[...]
    colsin = colsin_ref[...]
    cparts, sparts = [], []
    for rg in range(tm // GRID_W):
        r = i * (tm // GRID_W) + rg
        cparts.append(jnp.where(is_row_lane, rowcos_ref[pl.ds(r, 1), :], colcos))
        sparts.append(jnp.where(is_row_lane, rowsin_ref[pl.ds(r, 1), :], colsin))
    cos_t = jnp.concatenate(cparts, axis=0)
    sin_t = jnp.concatenate(sparts, axis=0)

    def rope(xs):
        swapped = jnp.where(first_half, pltpu.roll(xs, LANES - ROPE_PAIRS, 1), pltpu.roll(xs, ROPE_PAIRS, 1))
        return xs * cos_t + swapped * sin_t

    def head_mean_sq(t, bd):
        sq = t * t
        hi = sq.astype(BF16)
        lo = (sq - hi.astype(F32)).astype(BF16)
        s = jnp.dot(hi, bd, preferred_element_type=F32) + jnp.dot(lo, bd, preferred_element_type=F32)
        return s * (1.0 / HEAD_DIM)

    qz = z[:, :ATTN_W]
    q_ms = head_mean_sq(qz, bd_ref[...])
    qn = qz * lax.rsqrt(q_ms + EPS) * gq_ref[...]
    scale = 1.0 / math.sqrt(HEAD_DIM)
    for s in range(ATTN_W // LANES):
        sl = slice(s * LANES, (s + 1) * LANES)
        q_ref[0, :, sl] = (rope(qn[:, sl]) * scale).astype(BF16)

    kz = z[:, ATTN_W:ATTN_W + KV_W]
    k_ms = head_mean_sq(kz, bd_ref[:KV_W, :KV_W])
    kr = rope(kz * lax.rsqrt(k_ms + EPS) * gk_ref[...])
    kr_sw = pltpu.roll(kr, HEAD_DIM, 1)
    vz = z[:, ATTN_W + KV_W:ATTN_W + 2 * KV_W]
    vz_sw = pltpu.roll(vz, HEAD_DIM, 1)
    zero = jnp.zeros_like(kr)
    ones_lo = jnp.broadcast_to(jnp.where(low_head, 1.0, 0.0).astype(BF16), (tm, LANES))
    ones_hi = jnp.broadcast_to(jnp.where(low_head, 0.0, 1.0).astype(BF16), (tm, LANES))
    for g, (k_lo, k_hi, v_lo, v_hi) in enumerate(((kr, kr_sw, vz, vz_sw), (kr_sw, kr, vz_sw, vz))):
        k_ref[0, g, 0] = jnp.where(low_head, k_lo, zero).astype(BF16)
        k_ref[0, g, 1] = jnp.where(low_head, zero, k_hi).astype(BF16)
        v_ref[0, g, 0, :, :LANES] = jnp.where(low_head, v_lo, zero).astype(BF16)
        v_ref[0, g, 0, :, LANES:] = ones_lo
        v_ref[0, g, 1, :, :LANES] = jnp.where(low_head, zero, v_hi).astype(BF16)
        v_ref[0, g, 1, :, LANES:] = ones_hi

    u_ref[0] = z[:, ATTN_W + 2 * KV_W:]


def _proj_call(x, gmix, w_in, gq, gk, invf, bd):
    B, T, D = x.shape
    tm = _tiles(T)[0]
    in_w = w_in.shape[1]
    const = lambda b, i: (0, 0)
    return pl.pallas_call(
        functools.partial(_proj_body, tm=tm, seq=T),
        grid=(B, T // tm),
        in_specs=[
            pl.BlockSpec((1, tm, D), lambda b, i: (b, i, 0)),
            pl.BlockSpec((1, D), const),
            pl.BlockSpec((D, in_w), const),
            pl.BlockSpec((1, ATTN_W), const),
            pl.BlockSpec((1, KV_W), const),
            pl.BlockSpec((1, LANES), const),
            pl.BlockSpec((ATTN_W, ATTN_W), const),
        ],
        out_specs=[
            pl.BlockSpec((1, tm, ATTN_W), lambda b, i: (b, i, 0)),
            pl.BlockSpec((1, N_KV_HEADS, 2, tm, LANES), lambda b, i: (b, 0, 0, i, 0)),
            pl.BlockSpec((1, N_KV_HEADS, 2, tm, 2 * LANES), lambda b, i: (b, 0, 0, i, 0)),
            pl.BlockSpec((1, tm, POOL_W), lambda b, i: (b, i, 0)),
        ],
        out_shape=[
            jax.ShapeDtypeStruct((B, T, ATTN_W), BF16),
            jax.ShapeDtypeStruct((B, N_KV_HEADS, 2, T, LANES), BF16),
            jax.ShapeDtypeStruct((B, N_KV_HEADS, 2, T, 2 * LANES), BF16),
            jax.ShapeDtypeStruct((B, T, POOL_W), F32),
        ],
        scratch_shapes=[
            pltpu.VMEM((T // GRID_W, LANES), F32),
            pltpu.VMEM((T // GRID_W, LANES), F32),
            pltpu.VMEM((GRID_W, LANES), F32),
            pltpu.VMEM((GRID_W, LANES), F32),
        ],
        compiler_params=pltpu.CompilerParams(
            dimension_semantics=("arbitrary", "arbitrary"), vmem_limit_bytes=VMEM_LIMIT_BYTES),
        name="proj",
    )(x, gmix, w_in, gq, gk, invf, bd)


def _attn_body(q_ref, k_ref, v_ref, o_ref, s_ref, mt_ref, m_ref, acc_ref, *, tq, tk, seq):
    n_kt = seq // tk
    lane = lax.broadcasted_iota(jnp.int32, (1, LANES), 1)
    first_head_lane = lane < HEAD_DIM
    contract_last = (((1,), (1,)), ((), ()))

    def scores(slab, j):
        rows = pl.ds(pl.multiple_of(j * tk, tk), tk)
        qp = q_ref[0, :, slab * LANES:(slab + 1) * LANES]
        for head in range(2):
            s = lax.dot_general(qp, k_ref[0, 0, head, rows, :], contract_last, preferred_element_type=F32)
            s_ref[slab, head] = s
            mt_ref[slab, head] = jnp.broadcast_to(jnp.max(s, axis=1, keepdims=True), (tq, LANES))

    def update(slab, j):
        rows = pl.ds(pl.multiple_of(j * tk, tk), tk)
        alphas, pv = [], None
        for head in range(2):
            m_old = m_ref[slab, head]
            m_new = jnp.maximum(m_old, mt_ref[slab, head])
            m_ref[slab, head] = m_new
            alphas.append(jnp.exp(m_old - m_new))
            p = jnp.concatenate(
                [jnp.exp(s_ref[slab, head, :, c * LANES:(c + 1) * LANES] - m_new) for c in range(tk // LANES)],
                axis=1).astype(BF16)
            part = jnp.dot(p, v_ref[0, 0, head, rows, :], preferred_element_type=F32)
            pv = part if pv is None else pv + part
        alpha = jnp.where(first_head_lane, alphas[0], alphas[1])
        acc_ref[slab, :, :LANES] = alpha * acc_ref[slab, :, :LANES] + pv[:, :LANES]
        acc_ref[slab, :, LANES:] = alpha * acc_ref[slab, :, LANES:] + pv[:, LANES:]

    acc_ref[...] = jnp.zeros_like(acc_ref)
    m_ref[...] = jnp.full_like(m_ref, -jnp.inf)
    scores(0, 0)

    def step(j, carry):
        scores(1, j)
        update(0, j)
        scores(0, jnp.minimum(j + 1, n_kt - 1))
        update(1, j)
        return carry

    lax.fori_loop(0, n_kt, step, 0)
    for slab in range(2):
        o_ref[0, :, slab * LANES:(slab + 1) * LANES] = (
            acc_ref[slab, :, :LANES] / acc_ref[slab, :, LANES:]).astype(BF16)


def _attn_call(q, k, v):
    B, T, _ = q.shape
    _, tq, tk, _ = _tiles(T)
    return pl.pallas_call(
        functools.partial(_attn_body, tq=tq, tk=tk, seq=T),
        grid=(B, N_KV_HEADS, T // tq),
        in_specs=[
            pl.BlockSpec((1, tq, 2 * LANES), lambda b, g, i: (b, i, g)),
            pl.BlockSpec((1, 1, 2, T, LANES), lambda b, g, i: (b, g, 0, 0, 0)),
            pl.BlockSpec((1, 1, 2, T, 2 * LANES), lambda b, g, i: (b, g, 0, 0, 0)),
        ],
        out_specs=pl.BlockSpec((1, tq, 2 * LANES), lambda b, g, i: (b, i, g)),
        out_shape=jax.ShapeDtypeStruct((B, T, ATTN_W), BF16),
        scratch_shapes=[
            pltpu.VMEM((2, 2, tq, tk), F32),
            pltpu.VMEM((2, 2, tq, LANES), F32),
            pltpu.VMEM((2, 2, tq, LANES), F32),
            pltpu.VMEM((2, tq, 2 * LANES), F32),
        ],
        compiler_params=pltpu.CompilerParams(
            dimension_semantics=("arbitrary", "arbitrary", "arbitrary"), vmem_limit_bytes=VMEM_LIMIT_BYTES),
        name="attn",
    )(q, k, v)


def _mix_body(x_ref, o_ref, u_ref, uprev_ref, unext_ref, p_ref,
              wpool_ref, pscale_ref, wout_ref, gmlp_ref, wup_ref, wdown_ref,
              gple_ref, wgate_ref, wproj_ref, gfin_ref,
              y_ref, ext_ref, *, tm, seq, ff_chunk, final_norm):
    i = pl.program_id(1)
    n_tiles = pl.num_programs(1)

    ext_ref[0:POOL_HALO, :] = jnp.where(i > 0, uprev_ref[0], 0.0)
    ext_ref[POOL_HALO:POOL_HALO + tm, :] = u_ref[0]
    ext_ref[POOL_HALO + tm:, :] = jnp.where(i < n_tiles - 1, unext_ref[0], 0.0)

    t = i * tm + lax.broadcasted_iota(jnp.int32, (tm, 1), 0)
    mix = jnp.dot(o_ref[0], wout_ref[:ATTN_W, :], preferred_element_type=F32)
    for g, w in enumerate(POOL_WINDOWS):
        half = w // 2
        sl = slice(g * POOL_CH, (g + 1) * POOL_CH)
        wsum = ext_ref[POOL_HALO - half:POOL_HALO - half + tm, sl]
        for off in range(-half + 1, half):
            wsum = wsum + ext_ref[POOL_HALO + off:POOL_HALO + off + tm, sl]
        cnt = (jnp.minimum(t + half, seq) - jnp.maximum(t - half, 0)).astype(F32)
        d = (wsum / cnt - ext_ref[POOL_HALO:POOL_HALO + tm, sl]).astype(BF16)
        yg = jnp.dot(d, wpool_ref[g], preferred_element_type=F32) * pscale_ref[:, sl]
        mix = mix + jnp.dot(yg.astype(BF16), wout_ref[ATTN_W + g * POOL_CH:ATTN_W + (g + 1) * POOL_CH, :],
                            preferred_element_type=F32)
    h = x_ref[0] + mix

    m = _rms(h, gmlp_ref[...]).astype(BF16)
    d_ff = wup_ref.shape[1]
    mlp = None
    for c in range(d_ff // ff_chunk):
        cs = slice(c * ff_chunk, (c + 1) * ff_chunk)
        up = jnp.dot(m, wup_ref[:, cs], preferred_element_type=F32)
        act = jnp.square(jnp.maximum(up, 0.0)).astype(BF16)
        part = jnp.dot(act, wdown_ref[cs, :], preferred_element_type=F32)
        mlp = part if mlp is None else mlp + part
    h = h + mlp

    gate_in = _rms(h, gple_ref[...]).astype(BF16)
    gate = jax.nn.sigmoid(jnp.dot(gate_in, wgate_ref[...], preferred_element_type=F32))
    emb = jnp.dot(p_ref[0].astype(BF16), wproj_ref[...], preferred_element_type=F32)
    h = h + gate * emb

    y_ref[0] = _rms(h, gfin_ref[...]) if final_norm else h


def _mix_call(x, o, u, p, wpool, pscale, wout, gmlp, wup, wdown, gple, wgate, wproj, gfin, final_norm):
    B, T, D = x.shape
    tm = _tiles(T)[3]
    d_ff = wup.shape[1]
    ple = p.shape[-1]
    halo_blocks = tm // POOL_HALO
    n_halo = T // POOL_HALO

    def const(ndim):
        return lambda b, i: (0,) * ndim

    def resident(shape):
        return pl.BlockSpec(shape, const(len(shape)), pipeline_mode=pl.Buffered(1))

    tile = lambda w: pl.BlockSpec((1, tm, w), lambda b, i: (b, i, 0))
    return pl.pallas_call(
        functools.partial(_mix_body, tm=tm, seq=T, ff_chunk=min(1024, d_ff), final_norm=final_norm),
        grid=(B, T // tm),
        in_specs=[
            tile(D),
            tile(ATTN_W),
            tile(POOL_W),
            pl.BlockSpec((1, POOL_HALO, POOL_W), lambda b, i: (b, jnp.maximum(i * halo_blocks - 1, 0), 0)),
            pl.BlockSpec((1, POOL_HALO, POOL_W), lambda b, i: (b, jnp.minimum((i + 1) * halo_blocks, n_halo - 1), 0)),
            tile(ple),
            resident((len(POOL_WINDOWS), POOL_CH, POOL_CH)),
            resident((1, POOL_W)),
            resident((ATTN_W + POOL_W, D)),
            resident((1, D)),
            resident((D, d_ff)),
            resident((d_ff, D)),
            resident((1, D)),
            resident((D, D)),
            resident((ple, D)),
            resident((1, D)),
        ],
        out_specs=tile(D),
        out_shape=jax.ShapeDtypeStruct((B, T, D), F32),
        scratch_shapes=[pltpu.VMEM((tm + 2 * POOL_HALO, POOL_W), F32)],
        compiler_params=pltpu.CompilerParams(
            dimension_semantics=("arbitrary", "arbitrary"), vmem_limit_bytes=VMEM_LIMIT_BYTES),
        name="mix",
    )(x, o, u, u, u, p, wpool, pscale, wout, gmlp, wup, wdown, gple, wgate, wproj, gfin)


def _trunk(x, p, consts, layers, final_g):
    invf, bd = consts
    h = x
    for li, lw in enumerate(layers):
        q, k, v, u = _proj_call(h, lw["gmix"], lw["w_in"], lw["gq"], lw["gk"], invf, bd)
        o = _attn_call(q, k, v)
        h = _mix_call(h, o, u, p[li], lw["wpool"], lw["pscale"], lw["wout"], lw["gmlp"], lw["wup"], lw["wdown"],
                      lw["gple"], lw["wgate"], lw["wproj"], final_g, final_norm=(li == len(layers) - 1))
    return h


def kernel(x_prompt, x_sample, p_prompt, p_sample, norm_mix_g, w_in, q_norm_g, k_norm_g, w_pool, pool_scale, w_out, norm_mlp_g, w_up, w_down, norm_ple_g, w_ple_gate, w_ple_proj, final_norm_g):
    depth = w_in.shape[0]
    row = lambda g: g.reshape(1, -1).astype(F32)
    layers = []
    for li in range(depth):
        layers.append(dict(
            gmix=row(norm_mix_g[li]), w_in=w_in[li].astype(BF16),
            gq=row(jnp.tile(q_norm_g[li], N_Q_HEADS)), gk=row(jnp.tile(k_norm_g[li], N_KV_HEADS)),
            wpool=w_pool[li].astype(BF16), pscale=row(pool_scale[li]), wout=w_out[li].astype(BF16),
            gmlp=row(norm_mlp_g[li]), wup=w_up[li].astype(BF16), wdown=w_down[li].astype(BF16),
            gple=row(norm_ple_g[li]), wgate=w_ple_gate[li].astype(BF16), wproj=w_ple_proj[li].astype(BF16)))
    inv_freq = ROPE_THETA ** (-jnp.arange(ROPE_PAIRS, dtype=F32) / ROPE_PAIRS)
    invf = jnp.tile(inv_freq, LANES // ROPE_PAIRS).reshape(1, LANES)
    head_of = jnp.arange(ATTN_W) // HEAD_DIM
    bd = (head_of[:, None] == head_of[None, :]).astype(BF16)
    consts = (invf, bd)
    final_g = row(final_norm_g)
    y_prompt = _trunk(x_prompt, p_prompt, consts, layers, final_g)
    y_sample = _trunk(x_sample, p_sample, consts, layers, final_g)
    return (y_prompt, y_sample)
```

```python
import functools
import math

import jax
import jax.numpy as jnp
from jax import lax
from jax.experimental import pallas as pl
from jax.experimental.pallas import tpu as pltpu

F32 = jnp.float32
BF16 = jnp.bfloat16

HEAD_DIM = 64
N_Q_HEADS = 8
N_KV_HEADS = 2
ATTN_W = N_Q_HEADS * HEAD_DIM
KV_W = N_KV_HEADS * HEAD_DIM
POOL_WINDOWS = (2, 4, 8, 16)
POOL_CH = 128
POOL_W = POOL_CH * len(POOL_WINDOWS)
GRID_W = 64
ROPE_THETA = 10000.0
ROPE_PAIRS = HEAD_DIM // 4
EPS = 1e-6

LANES = 128
SUBLANES = 8
POOL_HALO = max(POOL_WINDOWS) // 2
assert POOL_HALO == SUBLANES

VMEM_LIMIT_BYTES = 56 * 1024 * 1024


def _tiles(T):
    tm = min(512, T)
    tp = min(2 * tm, T)
    assert T % tp == 0 and tm % GRID_W == 0
    return tp, tm, min(1024, T), tm


PROJ_ROW_CHUNK = 512
MIX_ROW_CHUNK = 512


def _rms(x, g):
    ms = jnp.mean(x * x, axis=-1, keepdims=True)
    return x * lax.rsqrt(ms + EPS) * g


def _proj_body(x_ref, gmix_ref, win_ref, gq_ref, gk_ref, invf_ref, bd_ref,
               q_ref, k_ref, v_ref, u_ref,
               rowcos_ref, rowsin_ref, colcos_ref, colsin_ref, *, tm, rc, seq):
    b = pl.program_id(0)
    i = pl.program_id(1)

    @pl.when((b == 0) & (i == 0))
    def _init_tables():
        lane = lax.broadcasted_iota(jnp.int32, (1, LANES), 1)
        invf = invf_ref[...]
        sign = jnp.where((lane & 16) == 0, -1.0, 1.0).astype(F32)
        rpos = lax.broadcasted_iota(jnp.int32, (seq // GRID_W, LANES), 0).astype(F32)
        rang = rpos * invf
        rowcos_ref[...] = jnp.cos(rang)
        rowsin_ref[...] = jnp.sin(rang) * sign
        cpos = lax.broadcasted_iota(jnp.int32, (GRID_W, LANES), 0).astype(F32)
        cang = cpos * invf
        colcos_ref[...] = jnp.cos(cang)
        colsin_ref[...] = jnp.sin(cang) * sign

    for c0 in range(0, tm, rc):
        _proj_rows(slice(c0, c0 + rc), i * (tm // GRID_W) + c0 // GRID_W,
                   x_ref, gmix_ref, win_ref, gq_ref, gk_ref, bd_ref, q_ref, k_ref, v_ref, u_ref,
                   rowcos_ref, rowsin_ref, colcos_ref, colsin_ref)


def _proj_rows(rows, grid_row0, x_ref, gmix_ref, win_ref, gq_ref, gk_ref, bd_ref, q_ref, k_ref, v_ref, u_ref,
               rowcos_ref, rowsin_ref, colcos_ref, colsin_ref):
    rc = rows.stop - rows.start
    lane = lax.broadcasted_iota(jnp.int32, (1, LANES), 1)
    is_row_lane = (lane & 32) == 0
    first_half = (lane & 16) == 0
    low_head = lane < HEAD_DIM

    x = x_ref[0, rows, :]
    a = _rms(x, gmix_ref[...]).astype(BF16)
    z = jnp.dot(a, win_ref[...], preferred_element_type=F32)

    colcos = colcos_ref[...]
    colsin = colsin_ref[...]
    cparts, sparts = [], []
    for rg in range(rc // GRID_W):
        r = grid_row0 + rg
        cparts.append(jnp.where(is_row_lane, rowcos_ref[pl.ds(r, 1), :], colcos))
        sparts.append(jnp.where(is_row_lane, rowsin_ref[pl.ds(r, 1), :], colsin))
    cos_t = jnp.concatenate(cparts, axis=0)
    sin_t = jnp.concatenate(sparts, axis=0)

    def rope(xs):
        swapped = jnp.where(first_half, pltpu.roll(xs, LANES - ROPE_PAIRS, 1), pltpu.roll(xs, ROPE_PAIRS, 1))
        return xs * cos_t + swapped * sin_t

    def head_mean_sq(t, bd):
        sq = t * t
        hi = sq.astype(BF16)
        lo = (sq - hi.astype(F32)).astype(BF16)
        s = jnp.dot(hi, bd, preferred_element_type=F32) + jnp.dot(lo, bd, preferred_element_type=F32)
        return s * (1.0 / HEAD_DIM)

    qz = z[:, :ATTN_W]
    q_ms = head_mean_sq(qz, bd_ref[...])
    qn = qz * lax.rsqrt(q_ms + EPS) * gq_ref[...]
    scale = math.log2(math.e) / math.sqrt(HEAD_DIM)
    for s in range(ATTN_W // LANES):
        sl = slice(s * LANES, (s + 1) * LANES)
        q_ref[0, rows, sl] = (rope(qn[:, sl]) * scale).astype(BF16)

    kz = z[:, ATTN_W:ATTN_W + KV_W]
    k_ms = head_mean_sq(kz, bd_ref[:KV_W, :KV_W])
    kr = rope(kz * lax.rsqrt(k_ms + EPS) * gk_ref[...])
    kr_sw = pltpu.roll(kr, HEAD_DIM, 1)
    vz = z[:, ATTN_W + KV_W:ATTN_W + 2 * KV_W]
    vz_sw = pltpu.roll(vz, HEAD_DIM, 1)
    zero = jnp.zeros_like(kr)
    ones_lo = jnp.broadcast_to(jnp.where(low_head, 1.0, 0.0).astype(BF16), (rc, LANES))
    ones_hi = jnp.broadcast_to(jnp.where(low_head, 0.0, 1.0).astype(BF16), (rc, LANES))
    for g, (k_lo, k_hi, v_lo, v_hi) in enumerate(((kr, kr_sw, vz, vz_sw), (kr_sw, kr, vz_sw, vz))):
        k_ref[0, g, 0, rows, :] = jnp.where(low_head, k_lo, zero).astype(BF16)
        k_ref[0, g, 1, rows, :] = jnp.where(low_head, zero, k_hi).astype(BF16)
        v_ref[0, g, 0, rows, :LANES] = jnp.where(low_head, v_lo, zero).astype(BF16)
        v_ref[0, g, 0, rows, LANES:] = ones_lo
        v_ref[0, g, 1, rows, :LANES] = jnp.where(low_head, zero, v_hi).astype(BF16)
        v_ref[0, g, 1, rows, LANES:] = ones_hi

    u_ref[0, rows, :] = z[:, ATTN_W + 2 * KV_W:]


def _proj_call(x, gmix, w_in, gq, gk, invf, bd):
    B, T, D = x.shape
    tm = _tiles(T)[0]
    in_w = w_in.shape[1]
    const = lambda b, i: (0, 0)
    return pl.pallas_call(
        functools.partial(_proj_body, tm=tm, rc=min(PROJ_ROW_CHUNK, tm), seq=T),
        grid=(B, T // tm),
        in_specs=[
            pl.BlockSpec((1, tm, D), lambda b, i: (b, i, 0)),
            pl.BlockSpec((1, D), const),
            pl.BlockSpec((D, in_w), const),
            pl.BlockSpec((1, ATTN_W), const),
            pl.BlockSpec((1, KV_W), const),
            pl.BlockSpec((1, LANES), const),
            pl.BlockSpec((ATTN_W, ATTN_W), const),
        ],
        out_specs=[
            pl.BlockSpec((1, tm, ATTN_W), lambda b, i: (b, i, 0)),
            pl.BlockSpec((1, N_KV_HEADS, 2, tm, LANES), lambda b, i: (b, 0, 0, i, 0)),
            pl.BlockSpec((1, N_KV_HEADS, 2, tm, 2 * LANES), lambda b, i: (b, 0, 0, i, 0)),
            pl.BlockSpec((1, tm, POOL_W), lambda b, i: (b, i, 0)),
        ],
        out_shape=[
            jax.ShapeDtypeStruct((B, T, ATTN_W), BF16),
            jax.ShapeDtypeStruct((B, N_KV_HEADS, 2, T, LANES), BF16),
            jax.ShapeDtypeStruct((B, N_KV_HEADS, 2, T, 2 * LANES), BF16),
            jax.ShapeDtypeStruct((B, T, POOL_W), F32),
        ],
        scratch_shapes=[
            pltpu.VMEM((T // GRID_W, LANES), F32),
            pltpu.VMEM((T // GRID_W, LANES), F32),
            pltpu.VMEM((GRID_W, LANES), F32),
            pltpu.VMEM((GRID_W, LANES), F32),
        ],
        compiler_params=pltpu.CompilerParams(
            dimension_semantics=("arbitrary", "arbitrary"), vmem_limit_bytes=VMEM_LIMIT_BYTES),
        name="proj",
    )(x, gmix, w_in, gq, gk, invf, bd)


def _attn_body(q_ref, qn_ref, k_ref, v_ref, o_ref, s_ref, mt_ref, m_ref, acc_ref, *, tq, tk, seq):
    n_kt = seq // tk
    lane = lax.broadcasted_iota(jnp.int32, (1, LANES), 1)
    first_head_lane = lane < HEAD_DIM
    contract_last = (((1,), (1,)), ((), ()))

    def scores(slab, qp, j):
        rows = pl.ds(pl.multiple_of(j * tk, tk), tk)
        for head in range(2):
            s = lax.dot_general(qp, k_ref[0, 0, head, rows, :], contract_last, preferred_element_type=F32)
            s_ref[slab, head] = s
            mt_ref[slab, head] = jnp.broadcast_to(jnp.max(s, axis=1, keepdims=True), (tq, LANES))

    def update(slab, j):
        rows = pl.ds(pl.multiple_of(j * tk, tk), tk)
        alphas, pv = [], None
        for head in range(2):
            m_old = m_ref[slab, head]
            m_new = jnp.maximum(m_old, mt_ref[slab, head])
            m_ref[slab, head] = m_new
            alphas.append(jnp.exp2(m_old - m_new))
            p = jnp.concatenate(
                [jnp.exp2(s_ref[slab, head, :, c * LANES:(c + 1) * LANES] - m_new) for c in range(tk // LANES)],
                axis=1).astype(BF16)
            part = jnp.dot(p, v_ref[0, 0, head, rows, :], preferred_element_type=F32)
            pv = part if pv is None else pv + part
        alpha = jnp.where(first_head_lane, alphas[0], alphas[1])
        acc_ref[slab, :, :LANES] = alpha * acc_ref[slab, :, :LANES] + pv[:, :LANES]
        acc_ref[slab, :, LANES:] = alpha * acc_ref[slab, :, LANES:] + pv[:, LANES:]

    acc_ref[...] = jnp.zeros_like(acc_ref)
    m_ref[...] = jnp.full_like(m_ref, -jnp.inf)

    @pl.when(pl.program_id(2) == 0)
    def _first_q_tile():
        scores(0, q_ref[0, :, :LANES], 0)

    def step(j, carry):
        last = j == n_kt - 1
        scores(1, q_ref[0, :, LANES:], j)
        update(0, j)
        scores(0, jnp.where(last, qn_ref[0, :, :LANES], q_ref[0, :, :LANES]), jnp.where(last, 0, j + 1))
        update(1, j)
        return carry

    lax.fori_loop(0, n_kt, step, 0, unroll=2)
    for slab in range(2):
        o_ref[0, :, slab * LANES:(slab + 1) * LANES] = (
            acc_ref[slab, :, :LANES] / acc_ref[slab, :, LANES:]).astype(BF16)


def _attn_call(q, k, v):
    B, T, _ = q.shape
    _, tq, tk, _ = _tiles(T)
    n_qt = T // tq
    return pl.pallas_call(
        functools.partial(_attn_body, tq=tq, tk=tk, seq=T),
        grid=(B, N_KV_HEADS, n_qt),
        in_specs=[
            pl.BlockSpec((1, tq, 2 * LANES), lambda b, g, i: (b, i, g)),
            pl.BlockSpec((1, tq, 2 * LANES), lambda b, g, i: (b, jnp.minimum(i + 1, n_qt - 1), g)),
            pl.BlockSpec((1, 1, 2, T, LANES), lambda b, g, i: (b, g, 0, 0, 0)),
            pl.BlockSpec((1, 1, 2, T, 2 * LANES), lambda b, g, i: (b, g, 0, 0, 0)),
        ],
        out_specs=pl.BlockSpec((1, tq, 2 * LANES), lambda b, g, i: (b, i, g)),
        out_shape=jax.ShapeDtypeStruct((B, T, ATTN_W), BF16),
        scratch_shapes=[
            pltpu.VMEM((2, 2, tq, tk), F32),
            pltpu.VMEM((2, 2, tq, LANES), F32),
            pltpu.VMEM((2, 2, tq, LANES), F32),
            pltpu.VMEM((2, tq, 2 * LANES), F32),
        ],
        compiler_params=pltpu.CompilerParams(
            dimension_semantics=("arbitrary", "arbitrary", "arbitrary"), vmem_limit_bytes=VMEM_LIMIT_BYTES),
        name="attn",
    )(q, q, k, v)


def _mix_body(x_ref, o_ref, u_ref, uprev_ref, unext_ref, p_ref,
              wpool_ref, pscale_ref, wout_ref, gmlp_ref, wup_ref, wdown_ref,
              gple_ref, wgate_ref, wproj_ref, gfin_ref,
              y_ref, ext_ref, *, tm, rc, seq, ff_chunk, final_norm):
    i = pl.program_id(1)
    n_tiles = pl.num_programs(1)

    ext_ref[0:POOL_HALO, :] = jnp.where(i > 0, uprev_ref[0], 0.0)
    ext_ref[POOL_HALO:POOL_HALO + tm, :] = u_ref[0]
    ext_ref[POOL_HALO + tm:, :] = jnp.where(i < n_tiles - 1, unext_ref[0], 0.0)

    for c0 in range(0, tm, rc):
        _mix_rows(c0, rc, i * tm + c0, x_ref, o_ref, p_ref, wpool_ref, pscale_ref, wout_ref, gmlp_ref, wup_ref,
                  wdown_ref, gple_ref, wgate_ref, wproj_ref, gfin_ref, y_ref, ext_ref,
                  seq=seq, ff_chunk=ff_chunk, final_norm=final_norm)


def _mix_rows(c0, rc, t0, x_ref, o_ref, p_ref, wpool_ref, pscale_ref, wout_ref, gmlp_ref, wup_ref, wdown_ref,
              gple_ref, wgate_ref, wproj_ref, gfin_ref, y_ref, ext_ref, *, seq, ff_chunk, final_norm):
    rows = slice(c0, c0 + rc)
    t = t0 + lax.broadcasted_iota(jnp.int32, (rc, 1), 0)
    pooled = []
    for g, w in enumerate(POOL_WINDOWS):
        half = w // 2
        sl = slice(g * POOL_CH, (g + 1) * POOL_CH)
        base = POOL_HALO + c0
        wsum = ext_ref[base - half:base - half + rc, sl]
        for off in range(-half + 1, half):
            wsum = wsum + ext_ref[base + off:base + off + rc, sl]
        cnt = (jnp.minimum(t + half, seq) - jnp.maximum(t - half, 0)).astype(F32)
        d = (wsum / cnt - ext_ref[base:base + rc, sl]).astype(BF16)
        yg = jnp.dot(d, wpool_ref[g], preferred_element_type=F32) * pscale_ref[:, sl]
        pooled.append(yg.astype(BF16))
    mixed = jnp.concatenate([o_ref[0, rows, :]] + pooled, axis=1)
    h = x_ref[0, rows, :] + jnp.dot(mixed, wout_ref[...], preferred_element_type=F32)

    m = _rms(h, gmlp_ref[...]).astype(BF16)
    d_ff = wup_ref.shape[1]
    mlp = None
    for c in range(d_ff // ff_chunk):
        cs = slice(c * ff_chunk, (c + 1) * ff_chunk)
        up = jnp.dot(m, wup_ref[:, cs], preferred_element_type=F32)
        act = jnp.square(jnp.maximum(up, 0.0)).astype(BF16)
        part = jnp.dot(act, wdown_ref[cs, :], preferred_element_type=F32)
        mlp = part if mlp is None else mlp + part
    h = h + mlp

    gate_in = _rms(h, gple_ref[...]).astype(BF16)
    gate = jax.nn.sigmoid(jnp.dot(gate_in, wgate_ref[...], preferred_element_type=F32))
    emb = jnp.dot(p_ref[0, rows, :].astype(BF16), wproj_ref[...], preferred_element_type=F32)
    h = h + gate * emb

    y_ref[0, rows, :] = _rms(h, gfin_ref[...]) if final_norm else h


def _mix_call(x, o, u, p, wpool, pscale, wout, gmlp, wup, wdown, gple, wgate, wproj, gfin, final_norm):
    B, T, D = x.shape
    tm = _tiles(T)[3]
    d_ff = wup.shape[1]
    ple = p.shape[-1]
    halo_blocks = tm // POOL_HALO
    n_halo = T // POOL_HALO

    def const(ndim):
        return lambda b, i: (0,) * ndim

    def resident(shape):
        return pl.BlockSpec(shape, const(len(shape)), pipeline_mode=pl.Buffered(1))

    tile = lambda w: pl.BlockSpec((1, tm, w), lambda b, i: (b, i, 0))
    return pl.pallas_call(
        functools.partial(_mix_body, tm=tm, rc=min(MIX_ROW_CHUNK, tm), seq=T, ff_chunk=min(1024, d_ff),
                          final_norm=final_norm),
        grid=(B, T // tm),
        in_specs=[
            tile(D),
            tile(ATTN_W),
            tile(POOL_W),
            pl.BlockSpec((1, POOL_HALO, POOL_W), lambda b, i: (b, jnp.maximum(i * halo_blocks - 1, 0), 0)),
            pl.BlockSpec((1, POOL_HALO, POOL_W), lambda b, i: (b, jnp.minimum((i + 1) * halo_blocks, n_halo - 1), 0)),
            tile(ple),
            resident((len(POOL_WINDOWS), POOL_CH, POOL_CH)),
            resident((1, POOL_W)),
            resident((ATTN_W + POOL_W, D)),
            resident((1, D)),
            resident((D, d_ff)),
            resident((d_ff, D)),
            resident((1, D)),
            resident((D, D)),
            resident((ple, D)),
            resident((1, D)),
        ],
        out_specs=tile(D),
        out_shape=jax.ShapeDtypeStruct((B, T, D), F32),
        scratch_shapes=[pltpu.VMEM((tm + 2 * POOL_HALO, POOL_W), F32)],
        compiler_params=pltpu.CompilerParams(
            dimension_semantics=("arbitrary", "arbitrary"), vmem_limit_bytes=VMEM_LIMIT_BYTES),
        name="mix",
    )(x, o, u, u, u, p, wpool, pscale, wout, gmlp, wup, wdown, gple, wgate, wproj, gfin)


def _trunk(x, p, consts, layers, final_g):
    invf, bd = consts
    h = x
    for li, lw in enumerate(layers):
        q, k, v, u = _proj_call(h, lw["gmix"], lw["w_in"], lw["gq"], lw["gk"], invf, bd)
        o = _attn_call(q, k, v)
        h = _mix_call(h, o, u, p[li], lw["wpool"], lw["pscale"], lw["wout"], lw["gmlp"], lw["wup"], lw["wdown"],
                      lw["gple"], lw["wgate"], lw["wproj"], final_g, final_norm=(li == len(layers) - 1))
    return h


def kernel(x_prompt, x_sample, p_prompt, p_sample, norm_mix_g, w_in, q_norm_g, k_norm_g, w_pool, pool_scale, w_out, norm_mlp_g, w_up, w_down, norm_ple_g, w_ple_gate, w_ple_proj, final_norm_g):
    depth = w_in.shape[0]
    row = lambda g: g.reshape(1, -1).astype(F32)
    layers = []
    for li in range(depth):
        layers.append(dict(
            gmix=row(norm_mix_g[li]), w_in=w_in[li].astype(BF16),
            gq=row(jnp.tile(q_norm_g[li], N_Q_HEADS)), gk=row(jnp.tile(k_norm_g[li], N_KV_HEADS)),
            wpool=w_pool[li].astype(BF16), pscale=row(pool_scale[li]), wout=w_out[li].astype(BF16),
            gmlp=row(norm_mlp_g[li]), wup=w_up[li].astype(BF16), wdown=w_down[li].astype(BF16),
            gple=row(norm_ple_g[li]), wgate=w_ple_gate[li].astype(BF16), wproj=w_ple_proj[li].astype(BF16)))
    inv_freq = ROPE_THETA ** (-jnp.arange(ROPE_PAIRS, dtype=F32) / ROPE_PAIRS)
    invf = jnp.tile(inv_freq, LANES // ROPE_PAIRS).reshape(1, LANES)
    head_of = jnp.arange(ATTN_W) // HEAD_DIM
    bd = (head_of[:, None] == head_of[None, :]).astype(BF16)
    consts = (invf, bd)
    final_g = row(final_norm_g)
    y_prompt = _trunk(x_prompt, p_prompt, consts, layers, final_g)
    y_sample = _trunk(x_sample, p_sample, consts, layers, final_g)
    return (y_prompt, y_sample)
```

```python
import functools
import math

import jax
import jax.numpy as jnp
from jax import lax
from jax.experimental import pallas as pl
from jax.experimental.pallas import tpu as pltpu

F32 = jnp.float32
BF16 = jnp.bfloat16

HEAD_DIM = 64
N_Q_HEADS = 8
N_KV_HEADS = 2
ATTN_W = N_Q_HEADS * HEAD_DIM
KV_W = N_KV_HEADS * HEAD_DIM
POOL_WINDOWS = (2, 4, 8, 16)
POOL_CH = 128
POOL_W = POOL_CH * len(POOL_WINDOWS)
GRID_W = 64
ROPE_THETA = 10000.0
ROPE_PAIRS = HEAD_DIM // 4
EPS = 1e-6

LANES = 128
SUBLANES = 8
POOL_HALO = max(POOL_WINDOWS) // 2
assert POOL_HALO == SUBLANES

VMEM_LIMIT_BYTES = 56 * 1024 * 1024


def _tiles(T):
    tm = min(512, T)
    tp = min(2 * tm, T)
    assert T % tp == 0 and tm % GRID_W == 0
    return tp, tm, min(1024, T), tm


PROJ_ROW_CHUNK = 512
BOUNDED_KEY_TILE = 256
BOUNDED_TILES_PER_TRIP = 32
SCORE_BOUND_LOG2 = 50.0


def _rms(x, g):
    ms = jnp.mean(x * x, axis=-1, keepdims=True)
    return x * lax.rsqrt(ms + EPS) * g


def _proj_body(x_ref, gmix_ref, win_ref, gq_ref, gk_ref, invf_ref, bd_ref,
               q_ref, k_ref, v_ref, u_ref,
               rowcos_ref, rowsin_ref, colcos_ref, colsin_ref, *, tm, rc, seq):
    b = pl.program_id(0)
    i = pl.program_id(1)

    @pl.when((b == 0) & (i == 0))
    def _init_tables():
        lane = lax.broadcasted_iota(jnp.int32, (1, LANES), 1)
        invf = invf_ref[...]
        sign = jnp.where((lane & 16) == 0, -1.0, 1.0).astype(F32)
        rpos = lax.broadcasted_iota(jnp.int32, (seq // GRID_W, LANES), 0).astype(F32)
        rang = rpos * invf
        rowcos_ref[...] = jnp.cos(rang)
        rowsin_ref[...] = jnp.sin(rang) * sign
        cpos = lax.broadcasted_iota(jnp.int32, (GRID_W, LANES), 0).astype(F32)
        cang = cpos * invf
        colcos_ref[...] = jnp.cos(cang)
        colsin_ref[...] = jnp.sin(cang) * sign

    for c0 in range(0, tm, rc):
        _proj_rows(slice(c0, c0 + rc), i * (tm // GRID_W) + c0 // GRID_W,
                   x_ref, gmix_ref, win_ref, gq_ref, gk_ref, bd_ref, q_ref, k_ref, v_ref, u_ref,
                   rowcos_ref, rowsin_ref, colcos_ref, colsin_ref)


def _proj_rows(rows, grid_row0, x_ref, gmix_ref, win_ref, gq_ref, gk_ref, bd_ref, q_ref, k_ref, v_ref, u_ref,
               rowcos_ref, rowsin_ref, colcos_ref, colsin_ref):
    rc = rows.stop - rows.start
    lane = lax.broadcasted_iota(jnp.int32, (1, LANES), 1)
    is_row_lane = (lane & 32) == 0
    first_half = (lane & 16) == 0
    low_head = lane < HEAD_DIM

    x = x_ref[0, rows, :]
    a = _rms(x, gmix_ref[...]).astype(BF16)
    z = jnp.dot(a, win_ref[...], preferred_element_type=F32)

    colcos = colcos_ref[...]
    colsin = colsin_ref[...]
    cparts, sparts = [], []
    for rg in range(rc // GRID_W):
        r = grid_row0 + rg
        cparts.append(jnp.where(is_row_lane, rowcos_ref[pl.ds(r, 1), :], colcos))
        sparts.append(jnp.where(is_row_lane, rowsin_ref[pl.ds(r, 1), :], colsin))
    cos_t = jnp.concatenate(cparts, axis=0)
    sin_t = jnp.concatenate(sparts, axis=0)

    def rope(xs):
        swapped = jnp.where(first_half, pltpu.roll(xs, LANES - ROPE_PAIRS, 1), pltpu.roll(xs, ROPE_PAIRS, 1))
        return xs * cos_t + swapped * sin_t

    def head_mean_sq(t, bd):
        sq = t * t
        hi = sq.astype(BF16)
        lo = (sq - hi.astype(F32)).astype(BF16)
        s = jnp.dot(hi, bd, preferred_element_type=F32) + jnp.dot(lo, bd, preferred_element_type=F32)
        return s * (1.0 / HEAD_DIM)

    qz = z[:, :ATTN_W]
    q_ms = head_mean_sq(qz, bd_ref[...])
    qn = qz * lax.rsqrt(q_ms + EPS) * gq_ref[...]
    scale = math.log2(math.e) / math.sqrt(HEAD_DIM)
    for s in range(ATTN_W // LANES):
        sl = slice(s * LANES, (s + 1) * LANES)
        q_ref[0, rows, sl] = (rope(qn[:, sl]) * scale).astype(BF16)

    kz = z[:, ATTN_W:ATTN_W + KV_W]
    k_ms = head_mean_sq(kz, bd_ref[:KV_W, :KV_W])
    kr = rope(kz * lax.rsqrt(k_ms + EPS) * gk_ref[...])
    kr_sw = pltpu.roll(kr, HEAD_DIM, 1)
    vz = z[:, ATTN_W + KV_W:ATTN_W + 2 * KV_W]
    vz_sw = pltpu.roll(vz, HEAD_DIM, 1)
    zero = jnp.zeros_like(kr)
    ones_lo = jnp.broadcast_to(jnp.where(low_head, 1.0, 0.0).astype(BF16), (rc, LANES))
    ones_hi = jnp.broadcast_to(jnp.where(low_head, 0.0, 1.0).astype(BF16), (rc, LANES))
    for g, (k_lo, k_hi, v_lo, v_hi) in enumerate(((kr, kr_sw, vz, vz_sw), (kr_sw, kr, vz_sw, vz))):
        k_ref[0, g, 0, rows, :] = jnp.where(low_head, k_lo, zero).astype(BF16)
        k_ref[0, g, 1, rows, :] = jnp.where(low_head, zero, k_hi).astype(BF16)
        v_ref[0, g, 0, rows, :LANES] = jnp.where(low_head, v_lo, zero).astype(BF16)
        v_ref[0, g, 0, rows, LANES:] = ones_lo
        v_ref[0, g, 1, rows, :LANES] = jnp.where(low_head, zero, v_hi).astype(BF16)
        v_ref[0, g, 1, rows, LANES:] = ones_hi

    u_ref[0, rows, :] = z[:, ATTN_W + 2 * KV_W:]


def _proj_call(x, gmix, w_in, gq, gk, invf, bd):
    B, T, D = x.shape
    tm = _tiles(T)[0]
    in_w = w_in.shape[1]
    const = lambda b, i: (0, 0)
    return pl.pallas_call(
        functools.partial(_proj_body, tm=tm, rc=min(PROJ_ROW_CHUNK, tm), seq=T),
        grid=(B, T // tm),
        in_specs=[
            pl.BlockSpec((1, tm, D), lambda b, i: (b, i, 0)),
            pl.BlockSpec((1, D), const),
            pl.BlockSpec((D, in_w), const),
            pl.BlockSpec((1, ATTN_W), const),
            pl.BlockSpec((1, KV_W), const),
            pl.BlockSpec((1, LANES), const),
            pl.BlockSpec((ATTN_W, ATTN_W), const),
        ],
        out_specs=[
            pl.BlockSpec((1, tm, ATTN_W), lambda b, i: (b, i, 0)),
            pl.BlockSpec((1, N_KV_HEADS, 2, tm, LANES), lambda b, i: (b, 0, 0, i, 0)),
            pl.BlockSpec((1, N_KV_HEADS, 2, tm, 2 * LANES), lambda b, i: (b, 0, 0, i, 0)),
            pl.BlockSpec((1, tm, POOL_W), lambda b, i: (b, i, 0)),
        ],
        out_shape=[
            jax.ShapeDtypeStruct((B, T, ATTN_W), BF16),
            jax.ShapeDtypeStruct((B, N_KV_HEADS, 2, T, LANES), BF16),
            jax.ShapeDtypeStruct((B, N_KV_HEADS, 2, T, 2 * LANES), BF16),
            jax.ShapeDtypeStruct((B, T, POOL_W), F32),
        ],
        scratch_shapes=[
            pltpu.VMEM((T // GRID_W, LANES), F32),
            pltpu.VMEM((T // GRID_W, LANES), F32),
            pltpu.VMEM((GRID_W, LANES), F32),
            pltpu.VMEM((GRID_W, LANES), F32),
        ],
        compiler_params=pltpu.CompilerParams(
            dimension_semantics=("arbitrary", "arbitrary"), vmem_limit_bytes=VMEM_LIMIT_BYTES),
        name="proj",
    )(x, gmix, w_in, gq, gk, invf, bd)


def _attn_body(q_ref, qn_ref, k_ref, v_ref, o_ref, s_ref, mt_ref, m_ref, acc_ref, *, tq, tk, seq):
    n_kt = seq // tk
    lane = lax.broadcasted_iota(jnp.int32, (1, LANES), 1)
    first_head_lane = lane < HEAD_DIM
    contract_last = (((1,), (1,)), ((), ()))

    def scores(slab, qp, j):
        rows = pl.ds(pl.multiple_of(j * tk, tk), tk)
        for head in range(2):
            s = lax.dot_general(qp, k_ref[0, 0, head, rows, :], contract_last, preferred_element_type=F32)
            s_ref[slab, head] = s
            mt_ref[slab, head] = jnp.broadcast_to(jnp.max(s, axis=1, keepdims=True), (tq, LANES))

    def update(slab, j):
        rows = pl.ds(pl.multiple_of(j * tk, tk), tk)
        alphas, pv = [], None
        for head in range(2):
            m_old = m_ref[slab, head]
            m_new = jnp.maximum(m_old, mt_ref[slab, head])
            m_ref[slab, head] = m_new
            alphas.append(jnp.exp2(m_old - m_new))
            p = jnp.concatenate(
                [jnp.exp2(s_ref[slab, head, :, c * LANES:(c + 1) * LANES] - m_new) for c in range(tk // LANES)],
                axis=1).astype(BF16)
            part = jnp.dot(p, v_ref[0, 0, head, rows, :], preferred_element_type=F32)
            pv = part if pv is None else pv + part
        alpha = jnp.where(first_head_lane, alphas[0], alphas[1])
        acc_ref[slab, :, :LANES] = alpha * acc_ref[slab, :, :LANES] + pv[:, :LANES]
        acc_ref[slab, :, LANES:] = alpha * acc_ref[slab, :, LANES:] + pv[:, LANES:]

    acc_ref[...] = jnp.zeros_like(acc_ref)
    m_ref[...] = jnp.full_like(m_ref, -jnp.inf)

    @pl.when(pl.program_id(2) == 0)
    def _first_q_tile():
        scores(0, q_ref[0, :, :LANES], 0)

    def step(j, carry):
        last = j == n_kt - 1
        scores(1, q_ref[0, :, LANES:], j)
        update(0, j)
        scores(0, jnp.where(last, qn_ref[0, :, :LANES], q_ref[0, :, :LANES]), jnp.where(last, 0, j + 1))
        update(1, j)
        return carry

    lax.fori_loop(0, n_kt, step, 0, unroll=2)
    for slab in range(2):
        o_ref[0, :, slab * LANES:(slab + 1) * LANES] = (
            acc_ref[slab, :, :LANES] / acc_ref[slab, :, LANES:]).astype(BF16)


def _attn_call(q, k, v):
    B, T, _ = q.shape
    _, tq, tk, _ = _tiles(T)
    n_qt = T // tq
    return pl.pallas_call(
        functools.partial(_attn_body, tq=tq, tk=tk, seq=T),
        grid=(B, N_KV_HEADS, n_qt),
        in_specs=[
            pl.BlockSpec((1, tq, 2 * LANES), lambda b, g, i: (b, i, g)),
            pl.BlockSpec((1, tq, 2 * LANES), lambda b, g, i: (b, jnp.minimum(i + 1, n_qt - 1), g)),
            pl.BlockSpec((1, 1, 2, T, LANES), lambda b, g, i: (b, g, 0, 0, 0)),
            pl.BlockSpec((1, 1, 2, T, 2 * LANES), lambda b, g, i: (b, g, 0, 0, 0)),
        ],
        out_specs=pl.BlockSpec((1, tq, 2 * LANES), lambda b, g, i: (b, i, g)),
        out_shape=jax.ShapeDtypeStruct((B, T, ATTN_W), BF16),
        scratch_shapes=[
            pltpu.VMEM((2, 2, tq, tk), F32),
            pltpu.VMEM((2, 2, tq, LANES), F32),
            pltpu.VMEM((2, 2, tq, LANES), F32),
            pltpu.VMEM((2, tq, 2 * LANES), F32),
        ],
        compiler_params=pltpu.CompilerParams(
            dimension_semantics=("arbitrary", "arbitrary", "arbitrary"), vmem_limit_bytes=VMEM_LIMIT_BYTES),
        name="attn",
    )(q, q, k, v)


def _attn_bounded_body(q_ref, k_ref, v_ref, o_ref, acc_ref, *, tq, tk, tiles_per_trip, seq):
    contract_last = (((1,), (1,)), ((), ()))
    acc_ref[...] = jnp.zeros_like(acc_ref)

    def trip(jj, carry):
        for slab in range(2):
            qp = q_ref[0, :, slab * LANES:(slab + 1) * LANES]
            total = None
            for t in range(tiles_per_trip):
                rows = pl.ds(pl.multiple_of((jj * tiles_per_trip + t) * tk, tk), tk)
                for head in range(2):
                    s = lax.dot_general(qp, k_ref[0, 0, head, rows, :], contract_last, preferred_element_type=F32)
                    term = jnp.dot(jnp.exp2(s).astype(BF16), v_ref[0, 0, head, rows, :], preferred_element_type=F32)
                    total = term if total is None else total + term
            acc_ref[slab] += total
        return carry

    lax.fori_loop(0, seq // (tk * tiles_per_trip), trip, 0)
    for slab in range(2):
        o_ref[0, :, slab * LANES:(slab + 1) * LANES] = (
            acc_ref[slab, :, :LANES] / acc_ref[slab, :, LANES:]).astype(BF16)


def _attn_bounded_call(q, k, v):
    B, T, _ = q.shape
    tq = _tiles(T)[1]
    tk = min(BOUNDED_KEY_TILE, T)
    return pl.pallas_call(
        functools.partial(_attn_bounded_body, tq=tq, tk=tk, tiles_per_trip=min(BOUNDED_TILES_PER_TRIP, T // tk), seq=T),
        grid=(B, N_KV_HEADS, T // tq),
        in_specs=[
            pl.BlockSpec((1, tq, 2 * LANES), lambda b, g, i: (b, i, g)),
            pl.BlockSpec((1, 1, 2, T, LANES), lambda b, g, i: (b, g, 0, 0, 0)),
            pl.BlockSpec((1, 1, 2, T, 2 * LANES), lambda b, g, i: (b, g, 0, 0, 0)),
        ],
        out_specs=pl.BlockSpec((1, tq, 2 * LANES), lambda b, g, i: (b, i, g)),
        out_shape=jax.ShapeDtypeStruct((B, T, ATTN_W), BF16),
        scratch_shapes=[pltpu.VMEM((2, tq, 2 * LANES), F32)],
        compiler_params=pltpu.CompilerParams(
            dimension_semantics=("arbitrary", "arbitrary", "arbitrary"), vmem_limit_bytes=VMEM_LIMIT_BYTES),
        name="attn_bounded",
    )(q, k, v)


def _mix_body(x_ref, o_ref, u_ref, uprev_ref, unext_ref, p_ref,
              wpool_ref, pscale_ref, wout_ref, gmlp_ref, wup_ref, wdown_ref,
              gple_ref, wgate_ref, wproj_ref, gfin_ref,
              y_ref, ext_ref, *, tm, seq, ff_chunk, final_norm):
    i = pl.program_id(1)
    n_tiles = pl.num_programs(1)

    ext_ref[0:POOL_HALO, :] = jnp.where(i > 0, uprev_ref[0], 0.0)
    ext_ref[POOL_HALO:POOL_HALO + tm, :] = u_ref[0]
    ext_ref[POOL_HALO + tm:, :] = jnp.where(i < n_tiles - 1, unext_ref[0], 0.0)

    t = i * tm + lax.broadcasted_iota(jnp.int32, (tm, 1), 0)
    pooled = []
    for g, w in enumerate(POOL_WINDOWS):
        half = w // 2
        sl = slice(g * POOL_CH, (g + 1) * POOL_CH)
        wsum = ext_ref[POOL_HALO - half:POOL_HALO - half + tm, sl]
        for off in range(-half + 1, half):
            wsum = wsum + ext_ref[POOL_HALO + off:POOL_HALO + off + tm, sl]
        cnt = (jnp.minimum(t + half, seq) - jnp.maximum(t - half, 0)).astype(F32)
        d = (wsum / cnt - ext_ref[POOL_HALO:POOL_HALO + tm, sl]).astype(BF16)
        yg = jnp.dot(d, wpool_ref[g], preferred_element_type=F32) * pscale_ref[:, sl]
        pooled.append(yg.astype(BF16))
    mixed = jnp.concatenate([o_ref[0]] + pooled, axis=1)
    h = x_ref[0] + jnp.dot(mixed, wout_ref[...], preferred_element_type=F32)

    m = _rms(h, gmlp_ref[...]).astype(BF16)
    d_ff = wup_ref.shape[1]
    mlp = None
    for c in range(d_ff // ff_chunk):
        cs = slice(c * ff_chunk, (c + 1) * ff_chunk)
        up = jnp.dot(m, wup_ref[:, cs], preferred_element_type=F32)
        act = jnp.square(jnp.maximum(up, 0.0)).astype(BF16)
        part = jnp.dot(act, wdown_ref[cs, :], preferred_element_type=F32)
        mlp = part if mlp is None else mlp + part
    h = h + mlp

    gate_in = _rms(h, gple_ref[...]).astype(BF16)
    gate = jax.nn.sigmoid(jnp.dot(gate_in, wgate_ref[...], preferred_element_type=F32))
    emb = jnp.dot(p_ref[0].astype(BF16), wproj_ref[...], preferred_element_type=F32)
    h = h + gate * emb

    y_ref[0] = _rms(h, gfin_ref[...]) if final_norm else h


def _mix_call(x, o, u, p, wpool, pscale, wout, gmlp, wup, wdown, gple, wgate, wproj, gfin, final_norm):
    B, T, D = x.shape
    tm = _tiles(T)[3]
    d_ff = wup.shape[1]
    ple = p.shape[-1]
    halo_blocks = tm // POOL_HALO
    n_halo = T // POOL_HALO

    def const(ndim):
        return lambda b, i: (0,) * ndim

    def resident(shape):
        return pl.BlockSpec(shape, const(len(shape)), pipeline_mode=pl.Buffered(1))

    tile = lambda w: pl.BlockSpec((1, tm, w), lambda b, i: (b, i, 0))
    return pl.pallas_call(
        functools.partial(_mix_body, tm=tm, seq=T, ff_chunk=min(1024, d_ff), final_norm=final_norm),
        grid=(B, T // tm),
        in_specs=[
            tile(D),
            tile(ATTN_W),
            tile(POOL_W),
            pl.BlockSpec((1, POOL_HALO, POOL_W), lambda b, i: (b, jnp.maximum(i * halo_blocks - 1, 0), 0)),
            pl.BlockSpec((1, POOL_HALO, POOL_W), lambda b, i: (b, jnp.minimum((i + 1) * halo_blocks, n_halo - 1), 0)),
            tile(ple),
            resident((len(POOL_WINDOWS), POOL_CH, POOL_CH)),
            resident((1, POOL_W)),
            resident((ATTN_W + POOL_W, D)),
            resident((1, D)),
            resident((D, d_ff)),
            resident((d_ff, D)),
            resident((1, D)),
            resident((D, D)),
            resident((ple, D)),
            resident((1, D)),
        ],
        out_specs=tile(D),
        out_shape=jax.ShapeDtypeStruct((B, T, D), F32),
        scratch_shapes=[pltpu.VMEM((tm + 2 * POOL_HALO, POOL_W), F32)],
        compiler_params=pltpu.CompilerParams(
            dimension_semantics=("arbitrary", "arbitrary"), vmem_limit_bytes=VMEM_LIMIT_BYTES),
        name="mix",
    )(x, o, u, u, u, p, wpool, pscale, wout, gmlp, wup, wdown, gple, wgate, wproj, gfin)


def _scores_bounded(q_gain, k_gain):
    bound = math.sqrt(HEAD_DIM) * math.log2(math.e) * jnp.max(jnp.abs(q_gain)) * jnp.max(jnp.abs(k_gain))
    return bound * 1.01 <= SCORE_BOUND_LOG2


def _trunk(x, p, consts, layers, final_g):
    invf, bd = consts
    h = x
    for li, lw in enumerate(layers):
        q, k, v, u = _proj_call(h, lw["gmix"], lw["w_in"], lw["gq"], lw["gk"], invf, bd)
        o = lax.cond(lw["scores_bounded"], _attn_bounded_call, _attn_call, q, k, v)
        h = _mix_call(h, o, u, p[li], lw["wpool"], lw["pscale"], lw["wout"], lw["gmlp"], lw["wup"], lw["wdown"],
                      lw["gple"], lw["wgate"], lw["wproj"], final_g, final_norm=(li == len(layers) - 1))
    return h


def kernel(x_prompt, x_sample, p_prompt, p_sample, norm_mix_g, w_in, q_norm_g, k_norm_g, w_pool, pool_scale, w_out, norm_mlp_g, w_up, w_down, norm_ple_g, w_ple_gate, w_ple_proj, final_norm_g):
    depth = w_in.shape[0]
    row = lambda g: g.reshape(1, -1).astype(F32)
    layers = []
    for li in range(depth):
        layers.append(dict(
            scores_bounded=_scores_bounded(q_norm_g[li], k_norm_g[li]),
            gmix=row(norm_mix_g[li]), w_in=w_in[li].astype(BF16),
            gq=row(jnp.tile(q_norm_g[li], N_Q_HEADS)), gk=row(jnp.tile(k_norm_g[li], N_KV_HEADS)),
            wpool=w_pool[li].astype(BF16), pscale=row(pool_scale[li]), wout=w_out[li].astype(BF16),
            gmlp=row(norm_mlp_g[li]), wup=w_up[li].astype(BF16), wdown=w_down[li].astype(BF16),
            gple=row(norm_ple_g[li]), wgate=w_ple_gate[li].astype(BF16), wproj=w_ple_proj[li].astype(BF16)))
    inv_freq = ROPE_THETA ** (-jnp.arange(ROPE_PAIRS, dtype=F32) / ROPE_PAIRS)
    invf = jnp.tile(inv_freq, LANES // ROPE_PAIRS).reshape(1, LANES)
    head_of = jnp.arange(ATTN_W) // HEAD_DIM
    bd = (head_of[:, None] == head_of[None, :]).astype(BF16)
    consts = (invf, bd)
    final_g = row(final_norm_g)
    y_prompt = _trunk(x_prompt, p_prompt, consts, layers, final_g)
    y_sample = _trunk(x_sample, p_sample, consts, layers, final_g)
    return (y_prompt, y_sample)
```

```python
import functools
import math

import jax
import jax.numpy as jnp
from jax import lax
from jax.experimental import pallas as pl
from jax.experimental.pallas import tpu as pltpu

F32 = jnp.float32
BF16 = jnp.bfloat16

HEAD_DIM = 64
N_Q_HEADS = 8
N_KV_HEADS = 2
ATTN_W = N_Q_HEADS * HEAD_DIM
KV_W = N_KV_HEADS * HEAD_DIM
POOL_WINDOWS = (2, 4, 8, 16)
POOL_CH = 128
POOL_W = POOL_CH * len(POOL_WINDOWS)
GRID_W = 64
ROPE_THETA = 10000.0
ROPE_PAIRS = HEAD_DIM // 4
EPS = 1e-6

LANES = 128
SUBLANES = 8
POOL_HALO = max(POOL_WINDOWS) // 2
assert POOL_HALO == SUBLANES

VMEM_LIMIT_BYTES = 56 * 1024 * 1024


def _tiles(T):
    tm = min(512, T)
    tp = min(2 * tm, T)
    assert T % tp == 0 and tm % GRID_W == 0
    return tp, tm, min(1024, T), tm


PROJ_ROW_CHUNK = 512
BOUNDED_KEY_TILE = 256
BOUNDED_TILES_PER_TRIP = 32
SCORE_BOUND_LOG2 = 50.0


def _rms(x, g):
    ms = jnp.mean(x * x, axis=-1, keepdims=True)
    return x * lax.rsqrt(ms + EPS) * g


def _proj_body(x_ref, gmix_ref, win_ref, gq_ref, gk_ref, invf_ref, bd_ref,
               q_ref, k_ref, v_ref, u_ref,
               rowcos_ref, rowsin_ref, colcos_ref, colsin_ref, *, tm, rc, seq):
    b = pl.program_id(0)
    i = pl.program_id(1)

    @pl.when((b == 0) & (i == 0))
    def _init_tables():
        lane = lax.broadcasted_iota(jnp.int32, (1, LANES), 1)
        invf = invf_ref[...]
        sign = jnp.where((lane & 16) == 0, -1.0, 1.0).astype(F32)
        rpos = lax.broadcasted_iota(jnp.int32, (seq // GRID_W, LANES), 0).astype(F32)
        rang = rpos * invf
        rowcos_ref[...] = jnp.cos(rang)
        rowsin_ref[...] = jnp.sin(rang) * sign
        cpos = lax.broadcasted_iota(jnp.int32, (GRID_W, LANES), 0).astype(F32)
        cang = cpos * invf
        colcos_ref[...] = jnp.cos(cang)
        colsin_ref[...] = jnp.sin(cang) * sign

    for c0 in range(0, tm, rc):
        _proj_rows(slice(c0, c0 + rc), i * (tm // GRID_W) + c0 // GRID_W,
                   x_ref, gmix_ref, win_ref, gq_ref, gk_ref, bd_ref, q_ref, k_ref, v_ref, u_ref,
                   rowcos_ref, rowsin_ref, colcos_ref, colsin_ref)


def _proj_rows(rows, grid_row0, x_ref, gmix_ref, win_ref, gq_ref, gk_ref, bd_ref, q_ref, k_ref, v_ref, u_ref,
               rowcos_ref, rowsin_ref, colcos_ref, colsin_ref):
    rc = rows.stop - rows.start
    lane = lax.broadcasted_iota(jnp.int32, (1, LANES), 1)
    is_row_lane = (lane & 32) == 0
    first_half = (lane & 16) == 0
    low_head = lane < HEAD_DIM

    x = x_ref[0, rows, :]
    a = _rms(x, gmix_ref[...]).astype(BF16)
    z = jnp.dot(a, win_ref[...], preferred_element_type=F32)

    colcos = colcos_ref[...]
    colsin = colsin_ref[...]
    cparts, sparts = [], []
    for rg in range(rc // GRID_W):
        r = grid_row0 + rg
        cparts.append(jnp.where(is_row_lane, rowcos_ref[pl.ds(r, 1), :], colcos))
        sparts.append(jnp.where(is_row_lane, rowsin_ref[pl.ds(r, 1), :], colsin))
    cos_t = jnp.concatenate(cparts, axis=0)
    sin_t = jnp.concatenate(sparts, axis=0)

    def rope(xs):
        swapped = jnp.where(first_half, pltpu.roll(xs, LANES - ROPE_PAIRS, 1), pltpu.roll(xs, ROPE_PAIRS, 1))
        return xs * cos_t + swapped * sin_t

    def head_mean_sq(t, bd):
        sq = t * t
        hi = sq.astype(BF16)
        lo = (sq - hi.astype(F32)).astype(BF16)
        s = jnp.dot(hi, bd, preferred_element_type=F32) + jnp.dot(lo, bd, preferred_element_type=F32)
        return s * (1.0 / HEAD_DIM)

    qz = z[:, :ATTN_W]
    q_ms = head_mean_sq(qz, bd_ref[...])
    qn = qz * lax.rsqrt(q_ms + EPS) * gq_ref[...]
    scale = math.log2(math.e) / math.sqrt(HEAD_DIM)
    for s in range(ATTN_W // LANES):
        sl = slice(s * LANES, (s + 1) * LANES)
        q_ref[0, rows, sl] = (rope(qn[:, sl]) * scale).astype(BF16)

    kz = z[:, ATTN_W:ATTN_W + KV_W]
    k_ms = head_mean_sq(kz, bd_ref[:KV_W, :KV_W])
    kr = rope(kz * lax.rsqrt(k_ms + EPS) * gk_ref[...])
    kr_sw = pltpu.roll(kr, HEAD_DIM, 1)
    vz = z[:, ATTN_W + KV_W:ATTN_W + 2 * KV_W]
    vz_sw = pltpu.roll(vz, HEAD_DIM, 1)
    zero = jnp.zeros_like(kr)
    ones_lo = jnp.broadcast_to(jnp.where(low_head, 1.0, 0.0).astype(BF16), (rc, LANES))
    ones_hi = jnp.broadcast_to(jnp.where(low_head, 0.0, 1.0).astype(BF16), (rc, LANES))
    for g, (k_lo, k_hi, v_lo, v_hi) in enumerate(((kr, kr_sw, vz, vz_sw), (kr_sw, kr, vz_sw, vz))):
        k_ref[0, g, 0, rows, :] = jnp.where(low_head, k_lo, zero).astype(BF16)
        k_ref[0, g, 1, rows, :] = jnp.where(low_head, zero, k_hi).astype(BF16)
        v_ref[0, g, 0, rows, :LANES] = jnp.where(low_head, v_lo, zero).astype(BF16)
        v_ref[0, g, 0, rows, LANES:] = ones_lo
        v_ref[0, g, 1, rows, :LANES] = jnp.where(low_head, zero, v_hi).astype(BF16)
        v_ref[0, g, 1, rows, LANES:] = ones_hi

    u_ref[0, rows, :] = z[:, ATTN_W + 2 * KV_W:]


def _proj_call(x, gmix, w_in, gq, gk, invf, bd):
    B, T, D = x.shape
    tm = _tiles(T)[0]
    in_w = w_in.shape[1]
    const = lambda b, i: (0, 0)
    return pl.pallas_call(
        functools.partial(_proj_body, tm=tm, rc=min(PROJ_ROW_CHUNK, tm), seq=T),
        grid=(B, T // tm),
        in_specs=[
            pl.BlockSpec((1, tm, D), lambda b, i: (b, i, 0)),
            pl.BlockSpec((1, D), const),
            pl.BlockSpec((D, in_w), const),
            pl.BlockSpec((1, ATTN_W), const),
            pl.BlockSpec((1, KV_W), const),
            pl.BlockSpec((1, LANES), const),
            pl.BlockSpec((ATTN_W, ATTN_W), const),
        ],
        out_specs=[
            pl.BlockSpec((1, tm, ATTN_W), lambda b, i: (b, i, 0)),
            pl.BlockSpec((1, N_KV_HEADS, 2, tm, LANES), lambda b, i: (b, 0, 0, i, 0)),
            pl.BlockSpec((1, N_KV_HEADS, 2, tm, 2 * LANES), lambda b, i: (b, 0, 0, i, 0)),
            pl.BlockSpec((1, tm, POOL_W), lambda b, i: (b, i, 0)),
        ],
        out_shape=[
            jax.ShapeDtypeStruct((B, T, ATTN_W), BF16),
            jax.ShapeDtypeStruct((B, N_KV_HEADS, 2, T, LANES), BF16),
            jax.ShapeDtypeStruct((B, N_KV_HEADS, 2, T, 2 * LANES), BF16),
            jax.ShapeDtypeStruct((B, T, POOL_W), F32),
        ],
        scratch_shapes=[
            pltpu.VMEM((T // GRID_W, LANES), F32),
            pltpu.VMEM((T // GRID_W, LANES), F32),
            pltpu.VMEM((GRID_W, LANES), F32),
            pltpu.VMEM((GRID_W, LANES), F32),
        ],
        compiler_params=pltpu.CompilerParams(
            dimension_semantics=("arbitrary", "arbitrary"), vmem_limit_bytes=VMEM_LIMIT_BYTES),
        name="proj",
    )(x, gmix, w_in, gq, gk, invf, bd)


def _pool_diffs(u_ref, uprev_ref, unext_ref, d_ref, ext_ref, row_block, n_row_blocks, *, rows, seq):
    ext_ref[0:POOL_HALO, :] = jnp.where(row_block == 0, 0.0, uprev_ref[0])
    ext_ref[POOL_HALO:POOL_HALO + rows, :] = u_ref[0]
    ext_ref[POOL_HALO + rows:, :] = jnp.where(row_block == n_row_blocks - 1, 0.0, unext_ref[0])

    r8 = lax.broadcasted_iota(jnp.int32, (SUBLANES, 1), 0)
    diffs = []
    for g, w in enumerate(POOL_WINDOWS):
        half = w // 2
        sl = slice(g * POOL_CH, (g + 1) * POOL_CH)
        wsum = ext_ref[POOL_HALO - half:POOL_HALO - half + rows, sl]
        for off in range(-half + 1, half):
            wsum = wsum + ext_ref[POOL_HALO + off:POOL_HALO + off + rows, sl]

        def clipped_mean(rows8, first_row):
            t = row_block * rows + first_row + r8
            cnt = (jnp.minimum(t + half, seq) - jnp.maximum(t - half, 0)).astype(F32)
            return rows8 / cnt

        mean = jnp.concatenate([
            clipped_mean(wsum[:SUBLANES], 0),
            wsum[SUBLANES:rows - SUBLANES] * (1.0 / w),
            clipped_mean(wsum[rows - SUBLANES:], rows - SUBLANES)], axis=0)
        diffs.append((mean - ext_ref[POOL_HALO:POOL_HALO + rows, sl]).astype(BF16))
        d_ref[0, :, sl] = diffs[-1]
    return diffs


def _ordered_after(x, deps):
    xw = pltpu.bitcast(x, jnp.uint32)
    for dep in deps:
        word = pltpu.bitcast(dep, jnp.uint32)
        zero = (word >> 16) >> 16
        xw = xw | jnp.concatenate([zero] * (xw.shape[0] // zero.shape[0]), axis=0)
    return pltpu.bitcast(xw, x.dtype)


def _pool_specs(T, tq):
    rows = tq // N_KV_HEADS
    halo_blocks = rows // POOL_HALO
    n_halo = T // POOL_HALO
    blk = lambda b, g, i: N_KV_HEADS * i + g
    in_specs = [
        pl.BlockSpec((1, rows, POOL_W), lambda b, g, i: (b, blk(b, g, i), 0)),
        pl.BlockSpec((1, POOL_HALO, POOL_W), lambda b, g, i: (b, jnp.maximum(blk(b, g, i) * halo_blocks - 1, 0), 0)),
        pl.BlockSpec((1, POOL_HALO, POOL_W),
                     lambda b, g, i: (b, jnp.minimum((blk(b, g, i) + 1) * halo_blocks, n_halo - 1), 0)),
    ]
    out_spec = pl.BlockSpec((1, rows, POOL_W), lambda b, g, i: (b, blk(b, g, i), 0))
    scratch = pltpu.VMEM((rows + 2 * POOL_HALO, POOL_W), F32)
    return rows, in_specs, out_spec, scratch


def _attn_body(q_ref, qn_ref, k_ref, v_ref, u_ref, uprev_ref, unext_ref, o_ref, d_ref,
               s_ref, mt_ref, m_ref, acc_ref, ext_ref, *, tq, tk, seq):
    n_kt = seq // tk
    lane = lax.broadcasted_iota(jnp.int32, (1, LANES), 1)
    first_head_lane = lane < HEAD_DIM
    contract_last = (((1,), (1,)), ((), ()))

    def scores(slab, qp, j):
        rows = pl.ds(pl.multiple_of(j * tk, tk), tk)
        for head in range(2):
            s = lax.dot_general(qp, k_ref[0, 0, head, rows, :], contract_last, preferred_element_type=F32)
            s_ref[slab, head] = s
            mt_ref[slab, head] = jnp.broadcast_to(jnp.max(s, axis=1, keepdims=True), (tq, LANES))

    def update(slab, j):
        rows = pl.ds(pl.multiple_of(j * tk, tk), tk)
        alphas, pv = [], None
        for head in range(2):
            m_old = m_ref[slab, head]
            m_new = jnp.maximum(m_old, mt_ref[slab, head])
            m_ref[slab, head] = m_new
            alphas.append(jnp.exp2(m_old - m_new))
            p = jnp.concatenate(
                [jnp.exp2(s_ref[slab, head, :, c * LANES:(c + 1) * LANES] - m_new) for c in range(tk // LANES)],
                axis=1).astype(BF16)
            part = jnp.dot(p, v_ref[0, 0, head, rows, :], preferred_element_type=F32)
            pv = part if pv is None else pv + part
        alpha = jnp.where(first_head_lane, alphas[0], alphas[1])
        acc_ref[slab, :, :LANES] = alpha * acc_ref[slab, :, :LANES] + pv[:, :LANES]
        acc_ref[slab, :, LANES:] = alpha * acc_ref[slab, :, LANES:] + pv[:, LANES:]

    pool_rows = tq // N_KV_HEADS
    _pool_diffs(u_ref, uprev_ref, unext_ref, d_ref, ext_ref, N_KV_HEADS * pl.program_id(2) + pl.program_id(1),
                seq // pool_rows, rows=pool_rows, seq=seq)

    acc_ref[...] = jnp.zeros_like(acc_ref)
    m_ref[...] = jnp.full_like(m_ref, -jnp.inf)

    @pl.when(pl.program_id(2) == 0)
    def _first_q_tile():
        scores(0, q_ref[0, :, :LANES], 0)

    def step(j, carry):
        last = j == n_kt - 1
        scores(1, q_ref[0, :, LANES:], j)
        update(0, j)
        scores(0, jnp.where(last, qn_ref[0, :, :LANES], q_ref[0, :, :LANES]), jnp.where(last, 0, j + 1))
        update(1, j)
        return carry

    lax.fori_loop(0, n_kt, step, 0, unroll=2)
    for slab in range(2):
        o_ref[0, :, slab * LANES:(slab + 1) * LANES] = (
            acc_ref[slab, :, :LANES] / acc_ref[slab, :, LANES:]).astype(BF16)


def _attn_call(q, k, v, u):
    B, T, _ = q.shape
    _, tq, tk, _ = _tiles(T)
    n_qt = T // tq
    _, pool_in, pool_out, pool_scratch = _pool_specs(T, tq)
    return pl.pallas_call(
        functools.partial(_attn_body, tq=tq, tk=tk, seq=T),
        grid=(B, N_KV_HEADS, n_qt),
        in_specs=[
            pl.BlockSpec((1, tq, 2 * LANES), lambda b, g, i: (b, i, g)),
            pl.BlockSpec((1, tq, 2 * LANES), lambda b, g, i: (b, jnp.minimum(i + 1, n_qt - 1), g)),
            pl.BlockSpec((1, 1, 2, T, LANES), lambda b, g, i: (b, g, 0, 0, 0)),
            pl.BlockSpec((1, 1, 2, T, 2 * LANES), lambda b, g, i: (b, g, 0, 0, 0)),
            *pool_in,
        ],
        out_specs=[pl.BlockSpec((1, tq, 2 * LANES), lambda b, g, i: (b, i, g)), pool_out],
        out_shape=[jax.ShapeDtypeStruct((B, T, ATTN_W), BF16), jax.ShapeDtypeStruct((B, T, POOL_W), BF16)],
        scratch_shapes=[
            pltpu.VMEM((2, 2, tq, tk), F32),
            pltpu.VMEM((2, 2, tq, LANES), F32),
            pltpu.VMEM((2, 2, tq, LANES), F32),
            pltpu.VMEM((2, tq, 2 * LANES), F32),
            pool_scratch,
        ],
        compiler_params=pltpu.CompilerParams(
            dimension_semantics=("arbitrary", "arbitrary", "arbitrary"), vmem_limit_bytes=VMEM_LIMIT_BYTES),
        name="attn",
    )(q, q, k, v, u, u, u)


def _attn_bounded_body(q_ref, k_ref, v_ref, u_ref, uprev_ref, unext_ref, o_ref, d_ref, acc_ref, ext_ref,
                       *, tq, tk, tiles_per_trip, seq):
    contract_last = (((1,), (1,)), ((), ()))
    pool_rows = tq // N_KV_HEADS
    diffs = _pool_diffs(u_ref, uprev_ref, unext_ref, d_ref, ext_ref,
                        N_KV_HEADS * pl.program_id(2) + pl.program_id(1), seq // pool_rows, rows=pool_rows, seq=seq)
    acc_ref[...] = jnp.zeros_like(acc_ref)

    half, quarter = pool_rows // 2, pool_rows // 4
    pieces = ([diffs[:2], [diffs[2][:half]], [diffs[2][half:]]]
              + [[diffs[3][r:r + quarter]] for r in range(0, pool_rows, quarter)])
    anchor_every = max(1, tiles_per_trip // (len(pieces) + 1))

    def trip(jj, carry):
        for slab in range(2):
            qp = q_ref[0, :, slab * LANES:(slab + 1) * LANES]
            total = None
            for t in range(tiles_per_trip):
                anchored = slab == 0 and t % anchor_every == 0 and 0 < t // anchor_every <= len(pieces)
                rows = pl.ds(pl.multiple_of((jj * tiles_per_trip + t) * tk, tk), tk)
                for head in range(2):
                    s = lax.dot_general(qp, k_ref[0, 0, head, rows, :], contract_last, preferred_element_type=F32)
                    vt = v_ref[0, 0, head, rows, :]
                    if anchored and head == 0:
                        ones = _ordered_after(vt[:, LANES:], pieces[t // anchor_every - 1])
                        vt = jnp.concatenate([vt[:, :LANES], ones], axis=1)
                    term = jnp.dot(jnp.exp2(s).astype(BF16), vt, preferred_element_type=F32)
                    total = term if total is None else total + term
            acc_ref[slab] += total
        return carry

    lax.fori_loop(0, seq // (tk * tiles_per_trip), trip, 0)
    for slab in range(2):
        o_ref[0, :, slab * LANES:(slab + 1) * LANES] = (
            acc_ref[slab, :, :LANES] / acc_ref[slab, :, LANES:]).astype(BF16)


def _attn_bounded_call(q, k, v, u):
    B, T, _ = q.shape
    tq = _tiles(T)[1]
    tk = min(BOUNDED_KEY_TILE, T)
    _, pool_in, pool_out, pool_scratch = _pool_specs(T, tq)
    return pl.pallas_call(
        functools.partial(_attn_bounded_body, tq=tq, tk=tk, tiles_per_trip=min(BOUNDED_TILES_PER_TRIP, T // tk), seq=T),
        grid=(B, N_KV_HEADS, T // tq),
        in_specs=[
            pl.BlockSpec((1, tq, 2 * LANES), lambda b, g, i: (b, i, g)),
            pl.BlockSpec((1, 1, 2, T, LANES), lambda b, g, i: (b, g, 0, 0, 0)),
            pl.BlockSpec((1, 1, 2, T, 2 * LANES), lambda b, g, i: (b, g, 0, 0, 0)),
            *pool_in,
        ],
        out_specs=[pl.BlockSpec((1, tq, 2 * LANES), lambda b, g, i: (b, i, g)), pool_out],
        out_shape=[jax.ShapeDtypeStruct((B, T, ATTN_W), BF16), jax.ShapeDtypeStruct((B, T, POOL_W), BF16)],
        scratch_shapes=[pltpu.VMEM((2, tq, 2 * LANES), F32), pool_scratch],
        compiler_params=pltpu.CompilerParams(
            dimension_semantics=("arbitrary", "arbitrary", "arbitrary"), vmem_limit_bytes=VMEM_LIMIT_BYTES),
        name="attn_bounded",
    )(q, k, v, u, u, u)


def _mix_body(x_ref, o_ref, d_ref, p_ref,
              wpool_ref, pscale_ref, wout_ref, gmlp_ref, wup_ref, wdown_ref,
              gple_ref, wgate_ref, wproj_ref, gfin_ref,
              y_ref, *, ff_chunk, final_norm):
    pooled = []
    for g in range(len(POOL_WINDOWS)):
        sl = slice(g * POOL_CH, (g + 1) * POOL_CH)
        yg = jnp.dot(d_ref[0, :, sl], wpool_ref[g], preferred_element_type=F32) * pscale_ref[:, sl]
        pooled.append(yg.astype(BF16))
    mixed = jnp.concatenate([o_ref[0]] + pooled, axis=1)
    h = x_ref[0] + jnp.dot(mixed, wout_ref[...], preferred_element_type=F32)

    m = _rms(h, gmlp_ref[...]).astype(BF16)
    d_ff = wup_ref.shape[1]
    mlp = None
    for c in range(d_ff // ff_chunk):
        cs = slice(c * ff_chunk, (c + 1) * ff_chunk)
        up = jnp.dot(m, wup_ref[:, cs], preferred_element_type=F32)
        act = jnp.square(jnp.maximum(up, 0.0)).astype(BF16)
        part = jnp.dot(act, wdown_ref[cs, :], preferred_element_type=F32)
        mlp = part if mlp is None else mlp + part
    h = h + mlp

    gate_in = _rms(h, gple_ref[...]).astype(BF16)
    gate = jax.nn.sigmoid(jnp.dot(gate_in, wgate_ref[...], preferred_element_type=F32))
    emb = jnp.dot(p_ref[0].astype(BF16), wproj_ref[...], preferred_element_type=F32)
    h = h + gate * emb

    y_ref[0] = _rms(h, gfin_ref[...]) if final_norm else h


def _mix_call(x, o, d, p, wpool, pscale, wout, gmlp, wup, wdown, gple, wgate, wproj, gfin, final_norm):
    B, T, D = x.shape
    tm = _tiles(T)[3]
    d_ff = wup.shape[1]
    ple = p.shape[-1]

    def const(ndim):
        return lambda b, i: (0,) * ndim

    def resident(shape):
        return pl.BlockSpec(shape, const(len(shape)), pipeline_mode=pl.Buffered(1))

    tile = lambda w: pl.BlockSpec((1, tm, w), lambda b, i: (b, i, 0))
    return pl.pallas_call(
        functools.partial(_mix_body, ff_chunk=min(1024, d_ff), final_norm=final_norm),
        grid=(B, T // tm),
        in_specs=[
            tile(D),
            tile(ATTN_W),
            tile(POOL_W),
            tile(ple),
            resident((len(POOL_WINDOWS), POOL_CH, POOL_CH)),
            resident((1, POOL_W)),
            resident((ATTN_W + POOL_W, D)),
            resident((1, D)),
            resident((D, d_ff)),
            resident((d_ff, D)),
            resident((1, D)),
            resident((D, D)),
            resident((ple, D)),
            resident((1, D)),
        ],
        out_specs=tile(D),
        out_shape=jax.ShapeDtypeStruct((B, T, D), F32),
        compiler_params=pltpu.CompilerParams(
            dimension_semantics=("arbitrary", "arbitrary"), vmem_limit_bytes=VMEM_LIMIT_BYTES),
        name="mix",
    )(x, o, d, p, wpool, pscale, wout, gmlp, wup, wdown, gple, wgate, wproj, gfin)


def _scores_bounded(q_gain, k_gain):
    bound = math.sqrt(HEAD_DIM) * math.log2(math.e) * jnp.max(jnp.abs(q_gain)) * jnp.max(jnp.abs(k_gain))
    return bound * 1.01 <= SCORE_BOUND_LOG2


def _trunk(x, p, consts, layers, final_g):
    invf, bd = consts
    h = x
    for li, lw in enumerate(layers):
        q, k, v, u = _proj_call(h, lw["gmix"], lw["w_in"], lw["gq"], lw["gk"], invf, bd)
        o, d = lax.cond(lw["scores_bounded"], _attn_bounded_call, _attn_call, q, k, v, u)
        h = _mix_call(h, o, d, p[li], lw["wpool"], lw["pscale"], lw["wout"], lw["gmlp"], lw["wup"], lw["wdown"],
                      lw["gple"], lw["wgate"], lw["wproj"], final_g, final_norm=(li == len(layers) - 1))
    return h


def kernel(x_prompt, x_sample, p_prompt, p_sample, norm_mix_g, w_in, q_norm_g, k_norm_g, w_pool, pool_scale, w_out, norm_mlp_g, w_up, w_down, norm_ple_g, w_ple_gate, w_ple_proj, final_norm_g):
    depth = w_in.shape[0]
    row = lambda g: g.reshape(1, -1).astype(F32)
    layers = []
    for li in range(depth):
        layers.append(dict(
            scores_bounded=_scores_bounded(q_norm_g[li], k_norm_g[li]),
            gmix=row(norm_mix_g[li]), w_in=w_in[li].astype(BF16),
            gq=row(jnp.tile(q_norm_g[li], N_Q_HEADS)), gk=row(jnp.tile(k_norm_g[li], N_KV_HEADS)),
            wpool=w_pool[li].astype(BF16), pscale=row(pool_scale[li]), wout=w_out[li].astype(BF16),
            gmlp=row(norm_mlp_g[li]), wup=w_up[li].astype(BF16), wdown=w_down[li].astype(BF16),
            gple=row(norm_ple_g[li]), wgate=w_ple_gate[li].astype(BF16), wproj=w_ple_proj[li].astype(BF16)))
    inv_freq = ROPE_THETA ** (-jnp.arange(ROPE_PAIRS, dtype=F32) / ROPE_PAIRS)
    invf = jnp.tile(inv_freq, LANES // ROPE_PAIRS).reshape(1, LANES)
    head_of = jnp.arange(ATTN_W) // HEAD_DIM
    bd = (head_of[:, None] == head_of[None, :]).astype(BF16)
    consts = (invf, bd)
    final_g = row(final_norm_g)
    y_prompt = _trunk(x_prompt, p_prompt, consts, layers, final_g)
    y_sample = _trunk(x_sample, p_sample, consts, layers, final_g)
    return (y_prompt, y_sample)
```

```python
import functools
import math

import jax
import jax.numpy as jnp
from jax import lax
from jax.experimental import pallas as pl
from jax.experimental.pallas import tpu as pltpu

F32 = jnp.float32
BF16 = jnp.bfloat16

HEAD_DIM = 64
N_Q_HEADS = 8
N_KV_HEADS = 2
ATTN_W = N_Q_HEADS * HEAD_DIM
KV_W = N_KV_HEADS * HEAD_DIM
POOL_WINDOWS = (2, 4, 8, 16)
POOL_CH = 128
POOL_W = POOL_CH * len(POOL_WINDOWS)
GRID_W = 64
ROPE_THETA = 10000.0
ROPE_PAIRS = HEAD_DIM // 4
EPS = 1e-6

LANES = 128
SUBLANES = 8
POOL_HALO = max(POOL_WINDOWS) // 2
assert POOL_HALO == SUBLANES

VMEM_LIMIT_BYTES = 56 * 1024 * 1024


def _tiles(T):
    tm = min(512, T)
    tp = min(4 * tm, T)
    assert T % tp == 0 and tm % GRID_W == 0
    return tp, tm, min(1024, T), tm


PROJ_ROW_CHUNK = 512
BOUNDED_Q_TILE = 512
BOUNDED_KEY_TILE = 256
BOUNDED_TILES_PER_TRIP = 32
SCORE_BOUND_LOG2 = 50.0


def _rms(x, g):
    ms = jnp.mean(x * x, axis=-1, keepdims=True)
    return x * lax.rsqrt(ms + EPS) * g


def _proj_body(x_ref, gmix_ref, win_ref, gq_ref, gk_ref, invf_ref, bd_ref,
               q_ref, k_ref, v_ref, u_ref,
               rowcos_ref, rowsin_ref, colcos_ref, colsin_ref, *, tm, rc, seq):
    b = pl.program_id(0)
    i = pl.program_id(1)

    @pl.when((b == 0) & (i == 0))
    def _init_tables():
        lane = lax.broadcasted_iota(jnp.int32, (1, LANES), 1)
        invf = invf_ref[...]
        sign = jnp.where((lane & 16) == 0, -1.0, 1.0).astype(F32)
        rpos = lax.broadcasted_iota(jnp.int32, (seq // GRID_W, LANES), 0).astype(F32)
        rang = rpos * invf
        rowcos_ref[...] = jnp.cos(rang)
        rowsin_ref[...] = jnp.sin(rang) * sign
        cpos = lax.broadcasted_iota(jnp.int32, (GRID_W, LANES), 0).astype(F32)
        cang = cpos * invf
        colcos_ref[...] = jnp.cos(cang)
        colsin_ref[...] = jnp.sin(cang) * sign

    for c0 in range(0, tm, rc):
        _proj_rows(slice(c0, c0 + rc), i * (tm // GRID_W) + c0 // GRID_W,
                   x_ref, gmix_ref, win_ref, gq_ref, gk_ref, bd_ref, q_ref, k_ref, v_ref, u_ref,
                   rowcos_ref, rowsin_ref, colcos_ref, colsin_ref)


def _proj_rows(rows, grid_row0, x_ref, gmix_ref, win_ref, gq_ref, gk_ref, bd_ref, q_ref, k_ref, v_ref, u_ref,
               rowcos_ref, rowsin_ref, colcos_ref, colsin_ref):
    rc = rows.stop - rows.start
    lane = lax.broadcasted_iota(jnp.int32, (1, LANES), 1)
    is_row_lane = (lane & 32) == 0
    first_half = (lane & 16) == 0
    low_head = lane < HEAD_DIM

    x = x_ref[0, rows, :]
    a = _rms(x, gmix_ref[...]).astype(BF16)
    z = jnp.dot(a, win_ref[...], preferred_element_type=F32)

    colcos = colcos_ref[...]
    colsin = colsin_ref[...]
    cparts, sparts = [], []
    for rg in range(rc // GRID_W):
        r = grid_row0 + rg
        cparts.append(jnp.where(is_row_lane, rowcos_ref[pl.ds(r, 1), :], colcos))
        sparts.append(jnp.where(is_row_lane, rowsin_ref[pl.ds(r, 1), :], colsin))
    cos_t = jnp.concatenate(cparts, axis=0)
    sin_t = jnp.concatenate(sparts, axis=0)

    def rope(xs):
        swapped = jnp.where(first_half, pltpu.roll(xs, LANES - ROPE_PAIRS, 1), pltpu.roll(xs, ROPE_PAIRS, 1))
        return xs * cos_t + swapped * sin_t

    def head_mean_sq(t, bd):
        sq = t * t
        hi = sq.astype(BF16)
        lo = (sq - hi.astype(F32)).astype(BF16)
        s = jnp.dot(hi, bd, preferred_element_type=F32) + jnp.dot(lo, bd, preferred_element_type=F32)
        return s * (1.0 / HEAD_DIM)

    qz = z[:, :ATTN_W]
    q_ms = head_mean_sq(qz, bd_ref[...])
    qn = qz * lax.rsqrt(q_ms + EPS) * gq_ref[...]
    scale = math.log2(math.e) / math.sqrt(HEAD_DIM)
    for s in range(ATTN_W // LANES):
        sl = slice(s * LANES, (s + 1) * LANES)
        q_ref[0, rows, sl] = (rope(qn[:, sl]) * scale).astype(BF16)

    kz = z[:, ATTN_W:ATTN_W + KV_W]
    k_ms = head_mean_sq(kz, bd_ref[:KV_W, :KV_W])
    kr = rope(kz * lax.rsqrt(k_ms + EPS) * gk_ref[...])
    kr_sw = pltpu.roll(kr, HEAD_DIM, 1)
    vz = z[:, ATTN_W + KV_W:ATTN_W + 2 * KV_W]
    vz_sw = pltpu.roll(vz, HEAD_DIM, 1)
    zero = jnp.zeros_like(kr)
    ones_lo = jnp.broadcast_to(jnp.where(low_head, 1.0, 0.0).astype(BF16), (rc, LANES))
    ones_hi = jnp.broadcast_to(jnp.where(low_head, 0.0, 1.0).astype(BF16), (rc, LANES))
    for g, (k_lo, k_hi, v_lo, v_hi) in enumerate(((kr, kr_sw, vz, vz_sw), (kr_sw, kr, vz_sw, vz))):
        k_ref[0, g, 0, rows, :] = jnp.where(low_head, k_lo, zero).astype(BF16)
        k_ref[0, g, 1, rows, :] = jnp.where(low_head, zero, k_hi).astype(BF16)
        v_ref[0, g, 0, rows, :LANES] = jnp.where(low_head, v_lo, zero).astype(BF16)
        v_ref[0, g, 0, rows, LANES:] = ones_lo
        v_ref[0, g, 1, rows, :LANES] = jnp.where(low_head, zero, v_hi).astype(BF16)
        v_ref[0, g, 1, rows, LANES:] = ones_hi

    u_ref[0, rows, :] = z[:, ATTN_W + 2 * KV_W:]


def _proj_call(x, gmix, w_in, gq, gk, invf, bd):
    B, T, D = x.shape
    tm = _tiles(T)[0]
    in_w = w_in.shape[1]
    const = lambda b, i: (0, 0)
    return pl.pallas_call(
        functools.partial(_proj_body, tm=tm, rc=min(PROJ_ROW_CHUNK, tm), seq=T),
        grid=(B, T // tm),
        in_specs=[
            pl.BlockSpec((1, tm, D), lambda b, i: (b, i, 0)),
            pl.BlockSpec((1, D), const),
            pl.BlockSpec((D, in_w), const),
            pl.BlockSpec((1, ATTN_W), const),
            pl.BlockSpec((1, KV_W), const),
            pl.BlockSpec((1, LANES), const),
            pl.BlockSpec((ATTN_W, ATTN_W), const),
        ],
        out_specs=[
            pl.BlockSpec((1, tm, ATTN_W), lambda b, i: (b, i, 0)),
            pl.BlockSpec((1, N_KV_HEADS, 2, tm, LANES), lambda b, i: (b, 0, 0, i, 0)),
            pl.BlockSpec((1, N_KV_HEADS, 2, tm, 2 * LANES), lambda b, i: (b, 0, 0, i, 0)),
            pl.BlockSpec((1, tm, POOL_W), lambda b, i: (b, i, 0)),
        ],
        out_shape=[
            jax.ShapeDtypeStruct((B, T, ATTN_W), BF16),
            jax.ShapeDtypeStruct((B, N_KV_HEADS, 2, T, LANES), BF16),
            jax.ShapeDtypeStruct((B, N_KV_HEADS, 2, T, 2 * LANES), BF16),
            jax.ShapeDtypeStruct((B, T, POOL_W), F32),
        ],
        scratch_shapes=[
            pltpu.VMEM((T // GRID_W, LANES), F32),
            pltpu.VMEM((T // GRID_W, LANES), F32),
            pltpu.VMEM((GRID_W, LANES), F32),
            pltpu.VMEM((GRID_W, LANES), F32),
        ],
        compiler_params=pltpu.CompilerParams(
            dimension_semantics=("arbitrary", "arbitrary"), vmem_limit_bytes=VMEM_LIMIT_BYTES),
        name="proj",
    )(x, gmix, w_in, gq, gk, invf, bd)


def _pool_diffs(u_ref, uprev_ref, unext_ref, d_ref, ext_ref, row_block, n_row_blocks, *, rows, seq):
    ext_ref[0:POOL_HALO, :] = jnp.where(row_block == 0, 0.0, uprev_ref[0])
    ext_ref[POOL_HALO:POOL_HALO + rows, :] = u_ref[0]
    ext_ref[POOL_HALO + rows:, :] = jnp.where(row_block == n_row_blocks - 1, 0.0, unext_ref[0])

    r8 = lax.broadcasted_iota(jnp.int32, (SUBLANES, 1), 0)
    diffs = []
    for g, w in enumerate(POOL_WINDOWS):
        half = w // 2
        sl = slice(g * POOL_CH, (g + 1) * POOL_CH)
        wsum = ext_ref[POOL_HALO - half:POOL_HALO - half + rows, sl]
        for off in range(-half + 1, half):
            wsum = wsum + ext_ref[POOL_HALO + off:POOL_HALO + off + rows, sl]

        def clipped_mean(rows8, first_row):
            t = row_block * rows + first_row + r8
            cnt = (jnp.minimum(t + half, seq) - jnp.maximum(t - half, 0)).astype(F32)
            return rows8 / cnt

        mean = jnp.concatenate([
            clipped_mean(wsum[:SUBLANES], 0),
            wsum[SUBLANES:rows - SUBLANES] * (1.0 / w),
            clipped_mean(wsum[rows - SUBLANES:], rows - SUBLANES)], axis=0)
        diffs.append((mean - ext_ref[POOL_HALO:POOL_HALO + rows, sl]).astype(BF16))
        d_ref[0, :, sl] = diffs[-1]
    return diffs


def _ordered_after(x, deps):
    xw = pltpu.bitcast(x, jnp.uint32)
    n = xw.shape[0]
    for dep in deps:
        word = pltpu.bitcast(dep, jnp.uint32)
        zero = (word >> 16) >> 16
        if zero.shape[0] < n:
            zero = jnp.concatenate([zero] * (n // zero.shape[0]), axis=0)
        for r in range(0, zero.shape[0], n):
            xw = xw | zero[r:r + n]
    return pltpu.bitcast(xw, x.dtype)


def _pool_specs(T, tq):
    rows = tq // N_KV_HEADS
    halo_blocks = rows // POOL_HALO
    n_halo = T // POOL_HALO
    blk = lambda b, g, i: N_KV_HEADS * i + g
    in_specs = [
        pl.BlockSpec((1, rows, POOL_W), lambda b, g, i: (b, blk(b, g, i), 0)),
        pl.BlockSpec((1, POOL_HALO, POOL_W), lambda b, g, i: (b, jnp.maximum(blk(b, g, i) * halo_blocks - 1, 0), 0)),
        pl.BlockSpec((1, POOL_HALO, POOL_W),
                     lambda b, g, i: (b, jnp.minimum((blk(b, g, i) + 1) * halo_blocks, n_halo - 1), 0)),
    ]
    out_spec = pl.BlockSpec((1, rows, POOL_W), lambda b, g, i: (b, blk(b, g, i), 0))
    scratch = pltpu.VMEM((rows + 2 * POOL_HALO, POOL_W), F32)
    return rows, in_specs, out_spec, scratch


def _attn_body(q_ref, qn_ref, k_ref, v_ref, u_ref, uprev_ref, unext_ref, o_ref, d_ref,
               s_ref, mt_ref, m_ref, acc_ref, ext_ref, *, tq, tk, seq):
    n_kt = seq // tk
    lane = lax.broadcasted_iota(jnp.int32, (1, LANES), 1)
    first_head_lane = lane < HEAD_DIM
    contract_last = (((1,), (1,)), ((), ()))

    def scores(slab, qp, j):
        rows = pl.ds(pl.multiple_of(j * tk, tk), tk)
        for head in range(2):
            s = lax.dot_general(qp, k_ref[0, 0, head, rows, :], contract_last, preferred_element_type=F32)
            s_ref[slab, head] = s
            mt_ref[slab, head] = jnp.broadcast_to(jnp.max(s, axis=1, keepdims=True), (tq, LANES))

    def update(slab, j):
        rows = pl.ds(pl.multiple_of(j * tk, tk), tk)
        alphas, pv = [], None
        for head in range(2):
            m_old = m_ref[slab, head]
            m_new = jnp.maximum(m_old, mt_ref[slab, head])
            m_ref[slab, head] = m_new
            alphas.append(jnp.exp2(m_old - m_new))
            p = jnp.concatenate(
                [jnp.exp2(s_ref[slab, head, :, c * LANES:(c + 1) * LANES] - m_new) for c in range(tk // LANES)],
                axis=1).astype(BF16)
            part = jnp.dot(p, v_ref[0, 0, head, rows, :], preferred_element_type=F32)
            pv = part if pv is None else pv + part
        alpha = jnp.where(first_head_lane, alphas[0], alphas[1])
        acc_ref[slab, :, :LANES] = alpha * acc_ref[slab, :, :LANES] + pv[:, :LANES]
        acc_ref[slab, :, LANES:] = alpha * acc_ref[slab, :, LANES:] + pv[:, LANES:]

    pool_rows = tq // N_KV_HEADS
    _pool_diffs(u_ref, uprev_ref, unext_ref, d_ref, ext_ref, N_KV_HEADS * pl.program_id(2) + pl.program_id(1),
                seq // pool_rows, rows=pool_rows, seq=seq)

    acc_ref[...] = jnp.zeros_like(acc_ref)
    m_ref[...] = jnp.full_like(m_ref, -jnp.inf)

    @pl.when(pl.program_id(2) == 0)
    def _first_q_tile():
        scores(0, q_ref[0, :, :LANES], 0)

    def step(j, carry):
        last = j == n_kt - 1
        scores(1, q_ref[0, :, LANES:], j)
        update(0, j)
        scores(0, jnp.where(last, qn_ref[0, :, :LANES], q_ref[0, :, :LANES]), jnp.where(last, 0, j + 1))
        update(1, j)
        return carry

    lax.fori_loop(0, n_kt, step, 0, unroll=2)
    for slab in range(2):
        o_ref[0, :, slab * LANES:(slab + 1) * LANES] = (
            acc_ref[slab, :, :LANES] / acc_ref[slab, :, LANES:]).astype(BF16)


def _attn_call(q, k, v, u):
    B, T, _ = q.shape
    _, tq, tk, _ = _tiles(T)
    n_qt = T // tq
    _, pool_in, pool_out, pool_scratch = _pool_specs(T, tq)
    return pl.pallas_call(
        functools.partial(_attn_body, tq=tq, tk=tk, seq=T),
        grid=(B, N_KV_HEADS, n_qt),
        in_specs=[
            pl.BlockSpec((1, tq, 2 * LANES), lambda b, g, i: (b, i, g)),
            pl.BlockSpec((1, tq, 2 * LANES), lambda b, g, i: (b, jnp.minimum(i + 1, n_qt - 1), g)),
            pl.BlockSpec((1, 1, 2, T, LANES), lambda b, g, i: (b, g, 0, 0, 0)),
            pl.BlockSpec((1, 1, 2, T, 2 * LANES), lambda b, g, i: (b, g, 0, 0, 0)),
            *pool_in,
        ],
        out_specs=[pl.BlockSpec((1, tq, 2 * LANES), lambda b, g, i: (b, i, g)), pool_out],
        out_shape=[jax.ShapeDtypeStruct((B, T, ATTN_W), BF16), jax.ShapeDtypeStruct((B, T, POOL_W), BF16)],
        scratch_shapes=[
            pltpu.VMEM((2, 2, tq, tk), F32),
            pltpu.VMEM((2, 2, tq, LANES), F32),
            pltpu.VMEM((2, 2, tq, LANES), F32),
            pltpu.VMEM((2, tq, 2 * LANES), F32),
            pool_scratch,
        ],
        compiler_params=pltpu.CompilerParams(
            dimension_semantics=("arbitrary", "arbitrary", "arbitrary"), vmem_limit_bytes=VMEM_LIMIT_BYTES),
        name="attn",
    )(q, q, k, v, u, u, u)


def _attn_bounded_body(q_ref, k_ref, v_ref, u_ref, uprev_ref, unext_ref, o_ref, d_ref, acc_ref, ext_ref,
                       *, tq, tk, tiles_per_trip, seq):
    contract_last = (((1,), (1,)), ((), ()))
    pool_rows = tq // N_KV_HEADS
    diffs = _pool_diffs(u_ref, uprev_ref, unext_ref, d_ref, ext_ref,
                        N_KV_HEADS * pl.program_id(2) + pl.program_id(1), seq // pool_rows, rows=pool_rows, seq=seq)
    acc_ref[...] = jnp.zeros_like(acc_ref)

    pieces = []
    for diff, w in zip(diffs, POOL_WINDOWS):
        step = pool_rows // max(1, w // 2)
        pieces += [[diff[r:r + step]] for r in range(0, pool_rows, step)]
    anchor_every = max(1, tiles_per_trip // (len(pieces) + 1))

    def trip(jj, carry):
        for slab in range(2):
            qp = q_ref[0, :, slab * LANES:(slab + 1) * LANES]
            total = None
            for t in range(tiles_per_trip):
                anchored = slab == 0 and t % anchor_every == 0 and 0 < t // anchor_every <= len(pieces)
                rows = pl.ds(pl.multiple_of((jj * tiles_per_trip + t) * tk, tk), tk)
                for head in range(2):
                    s = lax.dot_general(qp, k_ref[0, 0, head, rows, :], contract_last, preferred_element_type=F32)
                    vt = v_ref[0, 0, head, rows, :]
                    if anchored and head == 0:
                        ones = _ordered_after(vt[:, LANES:], pieces[t // anchor_every - 1])
                        vt = jnp.concatenate([vt[:, :LANES], ones], axis=1)
                    term = jnp.dot(jnp.exp2(s).astype(BF16), vt, preferred_element_type=F32)
                    total = term if total is None else total + term
            acc_ref[slab] += total
        return carry

    lax.fori_loop(0, seq // (tk * tiles_per_trip), trip, 0)
    for slab in range(2):
        o_ref[0, :, slab * LANES:(slab + 1) * LANES] = (
            acc_ref[slab, :, :LANES] / acc_ref[slab, :, LANES:]).astype(BF16)


def _attn_bounded_call(q, k, v, u):
    B, T, _ = q.shape
    tq = min(BOUNDED_Q_TILE, T)
    tk = min(BOUNDED_KEY_TILE, T)
    _, pool_in, pool_out, pool_scratch = _pool_specs(T, tq)
    return pl.pallas_call(
        functools.partial(_attn_bounded_body, tq=tq, tk=tk, tiles_per_trip=min(BOUNDED_TILES_PER_TRIP, T // tk), seq=T),
        grid=(B, N_KV_HEADS, T // tq),
        in_specs=[
            pl.BlockSpec((1, tq, 2 * LANES), lambda b, g, i: (b, i, g)),
            pl.BlockSpec((1, 1, 2, T, LANES), lambda b, g, i: (b, g, 0, 0, 0)),
            pl.BlockSpec((1, 1, 2, T, 2 * LANES), lambda b, g, i: (b, g, 0, 0, 0)),
            *pool_in,
        ],
        out_specs=[pl.BlockSpec((1, tq, 2 * LANES), lambda b, g, i: (b, i, g)), pool_out],
        out_shape=[jax.ShapeDtypeStruct((B, T, ATTN_W), BF16), jax.ShapeDtypeStruct((B, T, POOL_W), BF16)],
        scratch_shapes=[pltpu.VMEM((2, tq, 2 * LANES), F32), pool_scratch],
        compiler_params=pltpu.CompilerParams(
            dimension_semantics=("arbitrary", "arbitrary", "arbitrary"), vmem_limit_bytes=VMEM_LIMIT_BYTES),
        name="attn_bounded",
    )(q, k, v, u, u, u)


def _mix_body(x_ref, o_ref, d_ref, p_ref,
              wpool_ref, pscale_ref, wout_ref, gmlp_ref, wup_ref, wdown_ref,
              gple_ref, wgate_ref, wproj_ref, gfin_ref,
              y_ref, *, ff_chunk, final_norm):
    pooled = []
    for g in range(len(POOL_WINDOWS)):
        sl = slice(g * POOL_CH, (g + 1) * POOL_CH)
        yg = jnp.dot(d_ref[0, :, sl], wpool_ref[g], preferred_element_type=F32) * pscale_ref[:, sl]
        pooled.append(yg.astype(BF16))
    mixed = jnp.concatenate([o_ref[0]] + pooled, axis=1)
    h = x_ref[0] + jnp.dot(mixed, wout_ref[...], preferred_element_type=F32)

    m = _rms(h, gmlp_ref[...]).astype(BF16)
    d_ff = wup_ref.shape[1]
    mlp = None
    for c in range(d_ff // ff_chunk):
        cs = slice(c * ff_chunk, (c + 1) * ff_chunk)
        up = jnp.dot(m, wup_ref[:, cs], preferred_element_type=F32)
        act = jnp.square(jnp.maximum(up, 0.0)).astype(BF16)
        part = jnp.dot(act, wdown_ref[cs, :], preferred_element_type=F32)
        mlp = part if mlp is None else mlp + part
    h = h + mlp

    gate_in = _rms(h, gple_ref[...]).astype(BF16)
    gate = jax.nn.sigmoid(jnp.dot(gate_in, wgate_ref[...], preferred_element_type=F32))
    emb = jnp.dot(p_ref[0].astype(BF16), wproj_ref[...], preferred_element_type=F32)
    h = h + gate * emb

    y_ref[0] = _rms(h, gfin_ref[...]) if final_norm else h


def _mix_call(x, o, d, p, wpool, pscale, wout, gmlp, wup, wdown, gple, wgate, wproj, gfin, final_norm):
    B, T, D = x.shape
    tm = _tiles(T)[3]
    d_ff = wup.shape[1]
    ple = p.shape[-1]

    def const(ndim):
        return lambda b, i: (0,) * ndim

    def resident(shape):
        return pl.BlockSpec(shape, const(len(shape)), pipeline_mode=pl.Buffered(1))

    tile = lambda w: pl.BlockSpec((1, tm, w), lambda b, i: (b, i, 0))
    return pl.pallas_call(
        functools.partial(_mix_body, ff_chunk=min(2048, d_ff), final_norm=final_norm),
        grid=(B, T // tm),
        in_specs=[
            tile(D),
            tile(ATTN_W),
            tile(POOL_W),
            tile(ple),
            resident((len(POOL_WINDOWS), POOL_CH, POOL_CH)),
            resident((1, POOL_W)),
            resident((ATTN_W + POOL_W, D)),
            resident((1, D)),
            resident((D, d_ff)),
            resident((d_ff, D)),
            resident((1, D)),
            resident((D, D)),
            resident((ple, D)),
            resident((1, D)),
        ],
        out_specs=tile(D),
        out_shape=jax.ShapeDtypeStruct((B, T, D), F32),
        compiler_params=pltpu.CompilerParams(
            dimension_semantics=("arbitrary", "arbitrary"), vmem_limit_bytes=VMEM_LIMIT_BYTES),
        name="mix",
    )(x, o, d, p, wpool, pscale, wout, gmlp, wup, wdown, gple, wgate, wproj, gfin)


def _scores_bounded(q_gain, k_gain):
    bound = math.sqrt(HEAD_DIM) * math.log2(math.e) * jnp.max(jnp.abs(q_gain)) * jnp.max(jnp.abs(k_gain))
    return bound * 1.01 <= SCORE_BOUND_LOG2


def _trunk(x, p, consts, layers, final_g):
    invf, bd = consts
    h = x
    for li, lw in enumerate(layers):
        q, k, v, u = _proj_call(h, lw["gmix"], lw["w_in"], lw["gq"], lw["gk"], invf, bd)
        o, d = lax.cond(lw["scores_bounded"], _attn_bounded_call, _attn_call, q, k, v, u)
        h = _mix_call(h, o, d, p[li], lw["wpool"], lw["pscale"], lw["wout"], lw["gmlp"], lw["wup"], lw["wdown"],
                      lw["gple"], lw["wgate"], lw["wproj"], final_g, final_norm=(li == len(layers) - 1))
    return h


def kernel(x_prompt, x_sample, p_prompt, p_sample, norm_mix_g, w_in, q_norm_g, k_norm_g, w_pool, pool_scale, w_out, norm_mlp_g, w_up, w_down, norm_ple_g, w_ple_gate, w_ple_proj, final_norm_g):
    depth = w_in.shape[0]
    row = lambda g: g.reshape(1, -1).astype(F32)
    layers = []
    for li in range(depth):
        layers.append(dict(
            scores_bounded=_scores_bounded(q_norm_g[li], k_norm_g[li]),
            gmix=row(norm_mix_g[li]), w_in=w_in[li].astype(BF16),
            gq=row(jnp.tile(q_norm_g[li], N_Q_HEADS)), gk=row(jnp.tile(k_norm_g[li], N_KV_HEADS)),
            wpool=w_pool[li].astype(BF16), pscale=row(pool_scale[li]), wout=w_out[li].astype(BF16),
            gmlp=row(norm_mlp_g[li]), wup=w_up[li].astype(BF16), wdown=w_down[li].astype(BF16),
            gple=row(norm_ple_g[li]), wgate=w_ple_gate[li].astype(BF16), wproj=w_ple_proj[li].astype(BF16)))
    inv_freq = ROPE_THETA ** (-jnp.arange(ROPE_PAIRS, dtype=F32) / ROPE_PAIRS)
    invf = jnp.tile(inv_freq, LANES // ROPE_PAIRS).reshape(1, LANES)
    head_of = jnp.arange(ATTN_W) // HEAD_DIM
    bd = (head_of[:, None] == head_of[None, :]).astype(BF16)
    consts = (invf, bd)
    final_g = row(final_norm_g)
    y_prompt = _trunk(x_prompt, p_prompt, consts, layers, final_g)
    y_sample = _trunk(x_sample, p_sample, consts, layers, final_g)
    return (y_prompt, y_sample)
```

```python
import functools
import math

import jax
import jax.numpy as jnp
from jax import lax
from jax.experimental import pallas as pl
from jax.experimental.pallas import tpu as pltpu

F32 = jnp.float32
BF16 = jnp.bfloat16

HEAD_DIM = 64
N_Q_HEADS = 8
N_KV_HEADS = 2
ATTN_W = N_Q_HEADS * HEAD_DIM
KV_W = N_KV_HEADS * HEAD_DIM
POOL_WINDOWS = (2, 4, 8, 16)
POOL_CH = 128
POOL_W = POOL_CH * len(POOL_WINDOWS)
GRID_W = 64
ROPE_THETA = 10000.0
ROPE_PAIRS = HEAD_DIM // 4
EPS = 1e-6

LANES = 128
SUBLANES = 8
POOL_HALO = max(POOL_WINDOWS) // 2
assert POOL_HALO == SUBLANES

VMEM_LIMIT_BYTES = 56 * 1024 * 1024


def _tiles(T):
    tm = min(512, T)
    tp = min(4 * tm, T)
    assert T % tp == 0 and tm % GRID_W == 0
    return tp, tm, min(1024, T), tm


PROJ_ROW_CHUNK = 512
BOUNDED_Q_TILE = 512
BOUNDED_CHAIN_ROWS = 512 * 8192
BOUNDED_KEY_TILE = 256
BOUNDED_TILES_PER_TRIP = 32
SCORE_BOUND_LOG2 = 50.0


def _rms(x, g):
    ms = jnp.mean(x * x, axis=-1, keepdims=True)
    return x * lax.rsqrt(ms + EPS) * g


def _proj_body(x_ref, gmix_ref, win_ref, gq_ref, gk_ref, invf_ref, bd_ref,
               q_ref, k_ref, v_ref, u_ref,
               rowcos_ref, rowsin_ref, colcos_ref, colsin_ref, *, tm, rc, seq):
    b = pl.program_id(0)
    i = pl.program_id(1)

    @pl.when((b == 0) & (i == 0))
    def _init_tables():
        lane = lax.broadcasted_iota(jnp.int32, (1, LANES), 1)
        invf = invf_ref[...]
        sign = jnp.where((lane & 16) == 0, -1.0, 1.0).astype(F32)
        rpos = lax.broadcasted_iota(jnp.int32, (seq // GRID_W, LANES), 0).astype(F32)
        rang = rpos * invf
        rowcos_ref[...] = jnp.cos(rang)
        rowsin_ref[...] = jnp.sin(rang) * sign
        cpos = lax.broadcasted_iota(jnp.int32, (GRID_W, LANES), 0).astype(F32)
        cang = cpos * invf
        colcos_ref[...] = jnp.cos(cang)
        colsin_ref[...] = jnp.sin(cang) * sign

    for c0 in range(0, tm, rc):
        _proj_rows(slice(c0, c0 + rc), i * (tm // GRID_W) + c0 // GRID_W,
                   x_ref, gmix_ref, win_ref, gq_ref, gk_ref, bd_ref, q_ref, k_ref, v_ref, u_ref,
                   rowcos_ref, rowsin_ref, colcos_ref, colsin_ref)


def _proj_rows(rows, grid_row0, x_ref, gmix_ref, win_ref, gq_ref, gk_ref, bd_ref, q_ref, k_ref, v_ref, u_ref,
               rowcos_ref, rowsin_ref, colcos_ref, colsin_ref):
    rc = rows.stop - rows.start
    lane = lax.broadcasted_iota(jnp.int32, (1, LANES), 1)
    is_row_lane = (lane & 32) == 0
    first_half = (lane & 16) == 0
    low_head = lane < HEAD_DIM

    x = x_ref[0, rows, :]
    a = _rms(x, gmix_ref[...]).astype(BF16)
    z = jnp.dot(a, win_ref[...], preferred_element_type=F32)

    colcos = colcos_ref[...]
    colsin = colsin_ref[...]
    cparts, sparts = [], []
    for rg in range(rc // GRID_W):
        r = grid_row0 + rg
        cparts.append(jnp.where(is_row_lane, rowcos_ref[pl.ds(r, 1), :], colcos))
        sparts.append(jnp.where(is_row_lane, rowsin_ref[pl.ds(r, 1), :], colsin))
    cos_t = jnp.concatenate(cparts, axis=0)
    sin_t = jnp.concatenate(sparts, axis=0)

    def rope(xs):
        swapped = jnp.where(first_half, pltpu.roll(xs, LANES - ROPE_PAIRS, 1), pltpu.roll(xs, ROPE_PAIRS, 1))
        return xs * cos_t + swapped * sin_t

    def head_mean_sq(t, bd):
        sq = t * t
        hi = sq.astype(BF16)
        lo = (sq - hi.astype(F32)).astype(BF16)
        s = jnp.dot(hi, bd, preferred_element_type=F32) + jnp.dot(lo, bd, preferred_element_type=F32)
        return s * (1.0 / HEAD_DIM)

    qz = z[:, :ATTN_W]
    q_ms = head_mean_sq(qz, bd_ref[...])
    qn = qz * lax.rsqrt(q_ms + EPS) * gq_ref[...]
    scale = math.log2(math.e) / math.sqrt(HEAD_DIM)
    for s in range(ATTN_W // LANES):
        sl = slice(s * LANES, (s + 1) * LANES)
        q_ref[0, rows, sl] = (rope(qn[:, sl]) * scale).astype(BF16)

    kz = z[:, ATTN_W:ATTN_W + KV_W]
    k_ms = head_mean_sq(kz, bd_ref[:KV_W, :KV_W])
    kr = rope(kz * lax.rsqrt(k_ms + EPS) * gk_ref[...])
    kr_sw = pltpu.roll(kr, HEAD_DIM, 1)
    vz = z[:, ATTN_W + KV_W:ATTN_W + 2 * KV_W]
    vz_sw = pltpu.roll(vz, HEAD_DIM, 1)
    zero = jnp.zeros_like(kr)
    ones_lo = jnp.broadcast_to(jnp.where(low_head, 1.0, 0.0).astype(BF16), (rc, LANES))
    ones_hi = jnp.broadcast_to(jnp.where(low_head, 0.0, 1.0).astype(BF16), (rc, LANES))
    for g, (k_lo, k_hi, v_lo, v_hi) in enumerate(((kr, kr_sw, vz, vz_sw), (kr_sw, kr, vz_sw, vz))):
        k_ref[0, g, 0, rows, :] = jnp.where(low_head, k_lo, zero).astype(BF16)
        k_ref[0, g, 1, rows, :] = jnp.where(low_head, zero, k_hi).astype(BF16)
        v_ref[0, g, 0, rows, :LANES] = jnp.where(low_head, v_lo, zero).astype(BF16)
        v_ref[0, g, 0, rows, LANES:] = ones_lo
        v_ref[0, g, 1, rows, :LANES] = jnp.where(low_head, zero, v_hi).astype(BF16)
        v_ref[0, g, 1, rows, LANES:] = ones_hi

    u_ref[0, rows, :] = z[:, ATTN_W + 2 * KV_W:]


def _proj_call(x, gmix, w_in, gq, gk, invf, bd):
    B, T, D = x.shape
    tm = _tiles(T)[0]
    in_w = w_in.shape[1]
    const = lambda b, i: (0, 0)
    return pl.pallas_call(
        functools.partial(_proj_body, tm=tm, rc=min(PROJ_ROW_CHUNK, tm), seq=T),
        grid=(B, T // tm),
        in_specs=[
            pl.BlockSpec((1, tm, D), lambda b, i: (b, i, 0)),
            pl.BlockSpec((1, D), const),
            pl.BlockSpec((D, in_w), const),
            pl.BlockSpec((1, ATTN_W), const),
            pl.BlockSpec((1, KV_W), const),
            pl.BlockSpec((1, LANES), const),
            pl.BlockSpec((ATTN_W, ATTN_W), const),
        ],
        out_specs=[
            pl.BlockSpec((1, tm, ATTN_W), lambda b, i: (b, i, 0)),
            pl.BlockSpec((1, N_KV_HEADS, 2, tm, LANES), lambda b, i: (b, 0, 0, i, 0)),
            pl.BlockSpec((1, N_KV_HEADS, 2, tm, 2 * LANES), lambda b, i: (b, 0, 0, i, 0)),
            pl.BlockSpec((1, tm, POOL_W), lambda b, i: (b, i, 0)),
        ],
        out_shape=[
            jax.ShapeDtypeStruct((B, T, ATTN_W), BF16),
            jax.ShapeDtypeStruct((B, N_KV_HEADS, 2, T, LANES), BF16),
            jax.ShapeDtypeStruct((B, N_KV_HEADS, 2, T, 2 * LANES), BF16),
            jax.ShapeDtypeStruct((B, T, POOL_W), F32),
        ],
        scratch_shapes=[
            pltpu.VMEM((T // GRID_W, LANES), F32),
            pltpu.VMEM((T // GRID_W, LANES), F32),
            pltpu.VMEM((GRID_W, LANES), F32),
            pltpu.VMEM((GRID_W, LANES), F32),
        ],
        compiler_params=pltpu.CompilerParams(
            dimension_semantics=("arbitrary", "arbitrary"), vmem_limit_bytes=VMEM_LIMIT_BYTES),
        name="proj",
    )(x, gmix, w_in, gq, gk, invf, bd)


def _pool_diffs(u_ref, uprev_ref, unext_ref, d_ref, ext_ref, row_block, n_row_blocks, *, rows, seq):
    ext_ref[0:POOL_HALO, :] = jnp.where(row_block == 0, 0.0, uprev_ref[0])
    ext_ref[POOL_HALO:POOL_HALO + rows, :] = u_ref[0]
    ext_ref[POOL_HALO + rows:, :] = jnp.where(row_block == n_row_blocks - 1, 0.0, unext_ref[0])

    r8 = lax.broadcasted_iota(jnp.int32, (SUBLANES, 1), 0)
    diffs = []
    for g, w in enumerate(POOL_WINDOWS):
        half = w // 2
        sl = slice(g * POOL_CH, (g + 1) * POOL_CH)
        wsum = ext_ref[POOL_HALO - half:POOL_HALO - half + rows, sl]
        for off in range(-half + 1, half):
            wsum = wsum + ext_ref[POOL_HALO + off:POOL_HALO + off + rows, sl]

        def clipped_mean(rows8, first_row):
            t = row_block * rows + first_row + r8
            cnt = (jnp.minimum(t + half, seq) - jnp.maximum(t - half, 0)).astype(F32)
            return rows8 / cnt

        mean = jnp.concatenate([
            clipped_mean(wsum[:SUBLANES], 0),
            wsum[SUBLANES:rows - SUBLANES] * (1.0 / w),
            clipped_mean(wsum[rows - SUBLANES:], rows - SUBLANES)], axis=0)
        diffs.append((mean - ext_ref[POOL_HALO:POOL_HALO + rows, sl]).astype(BF16))
        d_ref[0, :, sl] = diffs[-1]
    return diffs


def _ordered_after(x, deps):
    xw = pltpu.bitcast(x, jnp.uint32)
    n = xw.shape[0]
    for dep in deps:
        word = pltpu.bitcast(dep, jnp.uint32)
        zero = (word >> 16) >> 16
        if zero.shape[0] < n:
            zero = jnp.concatenate([zero] * (n // zero.shape[0]), axis=0)
        for r in range(0, zero.shape[0], n):
            xw = xw | zero[r:r + n]
    return pltpu.bitcast(xw, x.dtype)


def _pool_specs(T, tq):
    rows = tq // N_KV_HEADS
    halo_blocks = rows // POOL_HALO
    n_halo = T // POOL_HALO
    blk = lambda b, g, i: N_KV_HEADS * i + g
    in_specs = [
        pl.BlockSpec((1, rows, POOL_W), lambda b, g, i: (b, blk(b, g, i), 0)),
        pl.BlockSpec((1, POOL_HALO, POOL_W), lambda b, g, i: (b, jnp.maximum(blk(b, g, i) * halo_blocks - 1, 0), 0)),
        pl.BlockSpec((1, POOL_HALO, POOL_W),
                     lambda b, g, i: (b, jnp.minimum((blk(b, g, i) + 1) * halo_blocks, n_halo - 1), 0)),
    ]
    out_spec = pl.BlockSpec((1, rows, POOL_W), lambda b, g, i: (b, blk(b, g, i), 0))
    scratch = pltpu.VMEM((rows + 2 * POOL_HALO, POOL_W), F32)
    return rows, in_specs, out_spec, scratch


def _attn_body(q_ref, qn_ref, k_ref, v_ref, u_ref, uprev_ref, unext_ref, o_ref, d_ref,
               s_ref, mt_ref, m_ref, acc_ref, ext_ref, *, tq, tk, seq):
    n_kt = seq // tk
    lane = lax.broadcasted_iota(jnp.int32, (1, LANES), 1)
    first_head_lane = lane < HEAD_DIM
    contract_last = (((1,), (1,)), ((), ()))

    def scores(slab, qp, j):
        rows = pl.ds(pl.multiple_of(j * tk, tk), tk)
        for head in range(2):
            s = lax.dot_general(qp, k_ref[0, 0, head, rows, :], contract_last, preferred_element_type=F32)
            s_ref[slab, head] = s
            mt_ref[slab, head] = jnp.broadcast_to(jnp.max(s, axis=1, keepdims=True), (tq, LANES))

    def update(slab, j):
        rows = pl.ds(pl.multiple_of(j * tk, tk), tk)
        alphas, pv = [], None
        for head in range(2):
            m_old = m_ref[slab, head]
            m_new = jnp.maximum(m_old, mt_ref[slab, head])
            m_ref[slab, head] = m_new
            alphas.append(jnp.exp2(m_old - m_new))
            p = jnp.concatenate(
                [jnp.exp2(s_ref[slab, head, :, c * LANES:(c + 1) * LANES] - m_new) for c in range(tk // LANES)],
                axis=1).astype(BF16)
            part = jnp.dot(p, v_ref[0, 0, head, rows, :], preferred_element_type=F32)
            pv = part if pv is None else pv + part
        alpha = jnp.where(first_head_lane, alphas[0], alphas[1])
        acc_ref[slab, :, :LANES] = alpha * acc_ref[slab, :, :LANES] + pv[:, :LANES]
        acc_ref[slab, :, LANES:] = alpha * acc_ref[slab, :, LANES:] + pv[:, LANES:]

    pool_rows = tq // N_KV_HEADS
    _pool_diffs(u_ref, uprev_ref, unext_ref, d_ref, ext_ref, N_KV_HEADS * pl.program_id(2) + pl.program_id(1),
                seq // pool_rows, rows=pool_rows, seq=seq)

    acc_ref[...] = jnp.zeros_like(acc_ref)
    m_ref[...] = jnp.full_like(m_ref, -jnp.inf)

    @pl.when(pl.program_id(2) == 0)
    def _first_q_tile():
        scores(0, q_ref[0, :, :LANES], 0)

    def step(j, carry):
        last = j == n_kt - 1
        scores(1, q_ref[0, :, LANES:], j)
        update(0, j)
        scores(0, jnp.where(last, qn_ref[0, :, :LANES], q_ref[0, :, :LANES]), jnp.where(last, 0, j + 1))
        update(1, j)
        return carry

    lax.fori_loop(0, n_kt, step, 0, unroll=2)
    for slab in range(2):
        o_ref[0, :, slab * LANES:(slab + 1) * LANES] = (
            acc_ref[slab, :, :LANES] / acc_ref[slab, :, LANES:]).astype(BF16)


def _attn_call(q, k, v, u):
    B, T, _ = q.shape
    _, tq, tk, _ = _tiles(T)
    n_qt = T // tq
    _, pool_in, pool_out, pool_scratch = _pool_specs(T, tq)
    return pl.pallas_call(
        functools.partial(_attn_body, tq=tq, tk=tk, seq=T),
        grid=(B, N_KV_HEADS, n_qt),
        in_specs=[
            pl.BlockSpec((1, tq, 2 * LANES), lambda b, g, i: (b, i, g)),
            pl.BlockSpec((1, tq, 2 * LANES), lambda b, g, i: (b, jnp.minimum(i + 1, n_qt - 1), g)),
            pl.BlockSpec((1, 1, 2, T, LANES), lambda b, g, i: (b, g, 0, 0, 0)),
            pl.BlockSpec((1, 1, 2, T, 2 * LANES), lambda b, g, i: (b, g, 0, 0, 0)),
            *pool_in,
        ],
        out_specs=[pl.BlockSpec((1, tq, 2 * LANES), lambda b, g, i: (b, i, g)), pool_out],
        out_shape=[jax.ShapeDtypeStruct((B, T, ATTN_W), BF16), jax.ShapeDtypeStruct((B, T, POOL_W), BF16)],
        scratch_shapes=[
            pltpu.VMEM((2, 2, tq, tk), F32),
            pltpu.VMEM((2, 2, tq, LANES), F32),
            pltpu.VMEM((2, 2, tq, LANES), F32),
            pltpu.VMEM((2, tq, 2 * LANES), F32),
            pool_scratch,
        ],
        compiler_params=pltpu.CompilerParams(
            dimension_semantics=("arbitrary", "arbitrary", "arbitrary"), vmem_limit_bytes=VMEM_LIMIT_BYTES),
        name="attn",
    )(q, q, k, v, u, u, u)


def _attn_bounded_body(q_ref, k_ref, v_ref, u_ref, uprev_ref, unext_ref, o_ref, d_ref, acc_ref, ext_ref,
                       *, tq, tk, tiles_per_trip, seq):
    contract_last = (((1,), (1,)), ((), ()))
    pool_rows = tq // N_KV_HEADS
    diffs = _pool_diffs(u_ref, uprev_ref, unext_ref, d_ref, ext_ref,
                        N_KV_HEADS * pl.program_id(2) + pl.program_id(1), seq // pool_rows, rows=pool_rows, seq=seq)
    acc_ref[...] = jnp.zeros_like(acc_ref)

    pieces = []
    for diff, w in zip(diffs, POOL_WINDOWS):
        step = pool_rows // max(1, w // 2)
        pieces += [[diff[r:r + step]] for r in range(0, pool_rows, step)]
    anchor_every = max(1, tiles_per_trip // (len(pieces) + 1))

    def trip(jj, carry):
        for slab in range(2):
            qp = q_ref[0, :, slab * LANES:(slab + 1) * LANES]
            total = None
            for t in range(tiles_per_trip):
                anchored = slab == 0 and t % anchor_every == 0 and 0 < t // anchor_every <= len(pieces)
                rows = pl.ds(pl.multiple_of((jj * tiles_per_trip + t) * tk, tk), tk)
                for head in range(2):
                    s = lax.dot_general(qp, k_ref[0, 0, head, rows, :], contract_last, preferred_element_type=F32)
                    vt = v_ref[0, 0, head, rows, :]
                    if anchored and head == 0:
                        ones = _ordered_after(vt[:, LANES:], pieces[t // anchor_every - 1])
                        vt = jnp.concatenate([vt[:, :LANES], ones], axis=1)
                    term = jnp.dot(jnp.exp2(s).astype(BF16), vt, preferred_element_type=F32)
                    total = term if total is None else total + term
            acc_ref[slab] += total
        return carry

    lax.fori_loop(0, seq // (tk * tiles_per_trip), trip, 0)
    for slab in range(2):
        o_ref[0, :, slab * LANES:(slab + 1) * LANES] = (
            acc_ref[slab, :, :LANES] / acc_ref[slab, :, LANES:]).astype(BF16)


def _attn_bounded_call(q, k, v, u):
    B, T, _ = q.shape
    tq = min(T, max(BOUNDED_Q_TILE, BOUNDED_CHAIN_ROWS // T))
    tk = min(BOUNDED_KEY_TILE, T)
    _, pool_in, pool_out, pool_scratch = _pool_specs(T, tq)
    return pl.pallas_call(
        functools.partial(_attn_bounded_body, tq=tq, tk=tk, tiles_per_trip=min(BOUNDED_TILES_PER_TRIP, T // tk), seq=T),
        grid=(B, N_KV_HEADS, T // tq),
        in_specs=[
            pl.BlockSpec((1, tq, 2 * LANES), lambda b, g, i: (b, i, g)),
            pl.BlockSpec((1, 1, 2, T, LANES), lambda b, g, i: (b, g, 0, 0, 0)),
            pl.BlockSpec((1, 1, 2, T, 2 * LANES), lambda b, g, i: (b, g, 0, 0, 0)),
            *pool_in,
        ],
        out_specs=[pl.BlockSpec((1, tq, 2 * LANES), lambda b, g, i: (b, i, g)), pool_out],
        out_shape=[jax.ShapeDtypeStruct((B, T, ATTN_W), BF16), jax.ShapeDtypeStruct((B, T, POOL_W), BF16)],
        scratch_shapes=[pltpu.VMEM((2, tq, 2 * LANES), F32), pool_scratch],
        compiler_params=pltpu.CompilerParams(
            dimension_semantics=("arbitrary", "arbitrary", "arbitrary"), vmem_limit_bytes=VMEM_LIMIT_BYTES),
        name="attn_bounded",
    )(q, k, v, u, u, u)


def _mix_body(x_ref, o_ref, d_ref, p_ref,
              wpool_ref, pscale_ref, wout_ref, gmlp_ref, wup_ref, wdown_ref,
              gple_ref, wgate_ref, wproj_ref, gfin_ref,
              y_ref, *, ff_chunk, final_norm):
    pooled = []
    for g in range(len(POOL_WINDOWS)):
        sl = slice(g * POOL_CH, (g + 1) * POOL_CH)
        yg = jnp.dot(d_ref[0, :, sl], wpool_ref[g], preferred_element_type=F32) * pscale_ref[:, sl]
        pooled.append(yg.astype(BF16))
    mixed = jnp.concatenate([o_ref[0]] + pooled, axis=1)
    h = x_ref[0] + jnp.dot(mixed, wout_ref[...], preferred_element_type=F32)

    m = _rms(h, gmlp_ref[...]).astype(BF16)
    d_ff = wup_ref.shape[1]
    mlp = None
    for c in range(d_ff // ff_chunk):
        cs = slice(c * ff_chunk, (c + 1) * ff_chunk)
        up = jnp.dot(m, wup_ref[:, cs], preferred_element_type=F32)
        act = jnp.square(jnp.maximum(up, 0.0)).astype(BF16)
        part = jnp.dot(act, wdown_ref[cs, :], preferred_element_type=F32)
        mlp = part if mlp is None else mlp + part
    h = h + mlp

    gate_in = _rms(h, gple_ref[...]).astype(BF16)
    gate = jax.nn.sigmoid(jnp.dot(gate_in, wgate_ref[...], preferred_element_type=F32))
    emb = jnp.dot(p_ref[0].astype(BF16), wproj_ref[...], preferred_element_type=F32)
    h = h + gate * emb

    y_ref[0] = _rms(h, gfin_ref[...]) if final_norm else h


def _mix_call(x, o, d, p, wpool, pscale, wout, gmlp, wup, wdown, gple, wgate, wproj, gfin, final_norm):
    B, T, D = x.shape
    tm = _tiles(T)[3]
    d_ff = wup.shape[1]
    ple = p.shape[-1]

    def const(ndim):
        return lambda b, i: (0,) * ndim

    def resident(shape):
        return pl.BlockSpec(shape, const(len(shape)), pipeline_mode=pl.Buffered(1))

    tile = lambda w: pl.BlockSpec((1, tm, w), lambda b, i: (b, i, 0))
    return pl.pallas_call(
        functools.partial(_mix_body, ff_chunk=min(2048, d_ff), final_norm=final_norm),
        grid=(B, T // tm),
        in_specs=[
            tile(D),
            tile(ATTN_W),
            tile(POOL_W),
            tile(ple),
            resident((len(POOL_WINDOWS), POOL_CH, POOL_CH)),
            resident((1, POOL_W)),
            resident((ATTN_W + POOL_W, D)),
            resident((1, D)),
            resident((D, d_ff)),
            resident((d_ff, D)),
            resident((1, D)),
            resident((D, D)),
            resident((ple, D)),
            resident((1, D)),
        ],
        out_specs=tile(D),
        out_shape=jax.ShapeDtypeStruct((B, T, D), F32),
        compiler_params=pltpu.CompilerParams(
            dimension_semantics=("arbitrary", "arbitrary"), vmem_limit_bytes=VMEM_LIMIT_BYTES),
        name="mix",
    )(x, o, d, p, wpool, pscale, wout, gmlp, wup, wdown, gple, wgate, wproj, gfin)


def _scores_bounded(q_gain, k_gain):
    bound = math.sqrt(HEAD_DIM) * math.log2(math.e) * jnp.max(jnp.abs(q_gain)) * jnp.max(jnp.abs(k_gain))
    return bound * 1.01 <= SCORE_BOUND_LOG2


def _trunk(x, p, consts, layers, final_g):
    invf, bd = consts
    h = x
    for li, lw in enumerate(layers):
        q, k, v, u = _proj_call(h, lw["gmix"], lw["w_in"], lw["gq"], lw["gk"], invf, bd)
        o, d = lax.cond(lw["scores_bounded"], _attn_bounded_call, _attn_call, q, k, v, u)
        h = _mix_call(h, o, d, p[li], lw["wpool"], lw["pscale"], lw["wout"], lw["gmlp"], lw["wup"], lw["wdown"],
                      lw["gple"], lw["wgate"], lw["wproj"], final_g, final_norm=(li == len(layers) - 1))
    return h


def kernel(x_prompt, x_sample, p_prompt, p_sample, norm_mix_g, w_in, q_norm_g, k_norm_g, w_pool, pool_scale, w_out, norm_mlp_g, w_up, w_down, norm_ple_g, w_ple_gate, w_ple_proj, final_norm_g):
    depth = w_in.shape[0]
    row = lambda g: g.reshape(1, -1).astype(F32)
    layers = []
    for li in range(depth):
        layers.append(dict(
            scores_bounded=_scores_bounded(q_norm_g[li], k_norm_g[li]),
            gmix=row(norm_mix_g[li]), w_in=w_in[li].astype(BF16),
            gq=row(jnp.tile(q_norm_g[li], N_Q_HEADS)), gk=row(jnp.tile(k_norm_g[li], N_KV_HEADS)),
            wpool=w_pool[li].astype(BF16), pscale=row(pool_scale[li]), wout=w_out[li].astype(BF16),
            gmlp=row(norm_mlp_g[li]), wup=w_up[li].astype(BF16), wdown=w_down[li].astype(BF16),
            gple=row(norm_ple_g[li]), wgate=w_ple_gate[li].astype(BF16), wproj=w_ple_proj[li].astype(BF16)))
    inv_freq = ROPE_THETA ** (-jnp.arange(ROPE_PAIRS, dtype=F32) / ROPE_PAIRS)
    invf = jnp.tile(inv_freq, LANES // ROPE_PAIRS).reshape(1, LANES)
    head_of = jnp.arange(ATTN_W) // HEAD_DIM
    bd = (head_of[:, None] == head_of[None, :]).astype(BF16)
    consts = (invf, bd)
    final_g = row(final_norm_g)
    y_prompt = _trunk(x_prompt, p_prompt, consts, layers, final_g)
    y_sample = _trunk(x_sample, p_sample, consts, layers, final_g)
    return (y_prompt, y_sample)
```

```python
import functools
import math

import jax
import jax.numpy as jnp
from jax import lax
from jax.experimental import pallas as pl
from jax.experimental.pallas import tpu as pltpu

F32 = jnp.float32
BF16 = jnp.bfloat16

HEAD_DIM = 64
N_Q_HEADS = 8
N_KV_HEADS = 2
ATTN_W = N_Q_HEADS * HEAD_DIM
KV_W = N_KV_HEADS * HEAD_DIM
POOL_WINDOWS = (2, 4, 8, 16)
POOL_CH = 128
POOL_W = POOL_CH * len(POOL_WINDOWS)
GRID_W = 64
ROPE_THETA = 10000.0
ROPE_PAIRS = HEAD_DIM // 4
EPS = 1e-6

LANES = 128
SUBLANES = 8
POOL_HALO = max(POOL_WINDOWS) // 2
assert POOL_HALO == SUBLANES

VMEM_LIMIT_BYTES = 56 * 1024 * 1024


def _tiles(T):
    tm = min(512, T)
    tp = min(4 * tm, T)
    assert T % tp == 0 and tm % GRID_W == 0
    return tp, tm, min(1024, T), tm


PROJ_ROW_CHUNK = 512
BOUNDED_Q_TILE = 512
BOUNDED_CHAIN_ROWS = 512 * 8192
BOUNDED_KEY_TILE = 256
BOUNDED_TILES_PER_TRIP = 32
SCORE_BOUND_LOG2 = 50.0


def _row_rsqrt(x):
    return lax.rsqrt(jnp.mean(x * x, axis=-1, keepdims=True) + EPS)


def _rms(x, g):
    return x * _row_rsqrt(x) * g


def _proj_body(x_ref, gmix_ref, win_ref, gq_ref, gk_ref, invf_ref, bd_ref,
               q_ref, k_ref, v_ref, u_ref,
               rowcos_ref, rowsin_ref, colcos_ref, colsin_ref, *, tm, rc, seq):
    b = pl.program_id(0)
    i = pl.program_id(1)

    @pl.when((b == 0) & (i == 0))
    def _init_tables():
        lane = lax.broadcasted_iota(jnp.int32, (1, LANES), 1)
        invf = invf_ref[...]
        sign = jnp.where((lane & 16) == 0, -1.0, 1.0).astype(F32)
        rpos = lax.broadcasted_iota(jnp.int32, (seq // GRID_W, LANES), 0).astype(F32)
        rang = rpos * invf
        rowcos_ref[...] = jnp.cos(rang)
        rowsin_ref[...] = jnp.sin(rang) * sign
        cpos = lax.broadcasted_iota(jnp.int32, (GRID_W, LANES), 0).astype(F32)
        cang = cpos * invf
        colcos_ref[...] = jnp.cos(cang)
        colsin_ref[...] = jnp.sin(cang) * sign

    for c0 in range(0, tm, rc):
        _proj_rows(slice(c0, c0 + rc), i * (tm // GRID_W) + c0 // GRID_W,
                   x_ref, gmix_ref, win_ref, gq_ref, gk_ref, bd_ref, q_ref, k_ref, v_ref, u_ref,
                   rowcos_ref, rowsin_ref, colcos_ref, colsin_ref)


def _proj_rows(rows, grid_row0, x_ref, gmix_ref, win_ref, gq_ref, gk_ref, bd_ref, q_ref, k_ref, v_ref, u_ref,
               rowcos_ref, rowsin_ref, colcos_ref, colsin_ref):
    rc = rows.stop - rows.start
    lane = lax.broadcasted_iota(jnp.int32, (1, LANES), 1)
    is_row_lane = (lane & 32) == 0
    first_half = (lane & 16) == 0
    low_head = lane < HEAD_DIM

    x = x_ref[0, rows, :]
    a = _rms(x, gmix_ref[...]).astype(BF16)
    z = jnp.dot(a, win_ref[...], preferred_element_type=F32)

    colcos = colcos_ref[...]
    colsin = colsin_ref[...]
    cparts, sparts = [], []
    for rg in range(rc // GRID_W):
        r = grid_row0 + rg
        cparts.append(jnp.where(is_row_lane, rowcos_ref[pl.ds(r, 1), :], colcos))
        sparts.append(jnp.where(is_row_lane, rowsin_ref[pl.ds(r, 1), :], colsin))
    cos_t = jnp.concatenate(cparts, axis=0)
    sin_t = jnp.concatenate(sparts, axis=0)

    def rope(xs):
        swapped = jnp.where(first_half, pltpu.roll(xs, LANES - ROPE_PAIRS, 1), pltpu.roll(xs, ROPE_PAIRS, 1))
        return xs * cos_t + swapped * sin_t

    def head_mean_sq(t, bd):
        sq = t * t
        hi = sq.astype(BF16)
        lo = (sq - hi.astype(F32)).astype(BF16)
        s = jnp.dot(hi, bd, preferred_element_type=F32) + jnp.dot(lo, bd, preferred_element_type=F32)
        return s * (1.0 / HEAD_DIM)

    qz = z[:, :ATTN_W]
    q_ms = head_mean_sq(qz, bd_ref[...])
    qn = qz * lax.rsqrt(q_ms + EPS) * gq_ref[...]
    scale = math.log2(math.e) / math.sqrt(HEAD_DIM)
    for s in range(ATTN_W // LANES):
        sl = slice(s * LANES, (s + 1) * LANES)
        q_ref[0, rows, sl] = (rope(qn[:, sl]) * scale).astype(BF16)

    kz = z[:, ATTN_W:ATTN_W + KV_W]
    k_ms = head_mean_sq(kz, bd_ref[:KV_W, :KV_W])
    kr = rope(kz * lax.rsqrt(k_ms + EPS) * gk_ref[...])
    kr_sw = pltpu.roll(kr, HEAD_DIM, 1)
    vz = z[:, ATTN_W + KV_W:ATTN_W + 2 * KV_W]
    vz_sw = pltpu.roll(vz, HEAD_DIM, 1)
    zero = jnp.zeros_like(kr)
    ones_lo = jnp.broadcast_to(jnp.where(low_head, 1.0, 0.0).astype(BF16), (rc, LANES))
    ones_hi = jnp.broadcast_to(jnp.where(low_head, 0.0, 1.0).astype(BF16), (rc, LANES))
    for g, (k_lo, k_hi, v_lo, v_hi) in enumerate(((kr, kr_sw, vz, vz_sw), (kr_sw, kr, vz_sw, vz))):
        k_ref[0, g, 0, rows, :] = jnp.where(low_head, k_lo, zero).astype(BF16)
        k_ref[0, g, 1, rows, :] = jnp.where(low_head, zero, k_hi).astype(BF16)
        v_ref[0, g, 0, rows, :LANES] = jnp.where(low_head, v_lo, zero).astype(BF16)
        v_ref[0, g, 0, rows, LANES:] = ones_lo
        v_ref[0, g, 1, rows, :LANES] = jnp.where(low_head, zero, v_hi).astype(BF16)
        v_ref[0, g, 1, rows, LANES:] = ones_hi

    u_ref[0, rows, :] = z[:, ATTN_W + 2 * KV_W:]


def _proj_call(x, gmix, w_in, gq, gk, invf, bd):
    B, T, D = x.shape
    tm = _tiles(T)[0]
    in_w = w_in.shape[1]
    const = lambda b, i: (0, 0)
    return pl.pallas_call(
        functools.partial(_proj_body, tm=tm, rc=min(PROJ_ROW_CHUNK, tm), seq=T),
        grid=(B, T // tm),
        in_specs=[
            pl.BlockSpec((1, tm, D), lambda b, i: (b, i, 0)),
            pl.BlockSpec((1, D), const),
            pl.BlockSpec((D, in_w), const),
            pl.BlockSpec((1, ATTN_W), const),
            pl.BlockSpec((1, KV_W), const),
            pl.BlockSpec((1, LANES), const),
            pl.BlockSpec((ATTN_W, ATTN_W), const),
        ],
        out_specs=[
            pl.BlockSpec((1, tm, ATTN_W), lambda b, i: (b, i, 0)),
            pl.BlockSpec((1, N_KV_HEADS, 2, tm, LANES), lambda b, i: (b, 0, 0, i, 0)),
            pl.BlockSpec((1, N_KV_HEADS, 2, tm, 2 * LANES), lambda b, i: (b, 0, 0, i, 0)),
            pl.BlockSpec((1, tm, POOL_W), lambda b, i: (b, i, 0)),
        ],
        out_shape=[
            jax.ShapeDtypeStruct((B, T, ATTN_W), BF16),
            jax.ShapeDtypeStruct((B, N_KV_HEADS, 2, T, LANES), BF16),
            jax.ShapeDtypeStruct((B, N_KV_HEADS, 2, T, 2 * LANES), BF16),
            jax.ShapeDtypeStruct((B, T, POOL_W), F32),
        ],
        scratch_shapes=[
            pltpu.VMEM((T // GRID_W, LANES), F32),
            pltpu.VMEM((T // GRID_W, LANES), F32),
            pltpu.VMEM((GRID_W, LANES), F32),
            pltpu.VMEM((GRID_W, LANES), F32),
        ],
        compiler_params=pltpu.CompilerParams(
            dimension_semantics=("arbitrary", "arbitrary"), vmem_limit_bytes=VMEM_LIMIT_BYTES),
        name="proj",
    )(x, gmix, w_in, gq, gk, invf, bd)


def _pool_diffs(u_ref, uprev_ref, unext_ref, d_ref, ext_ref, row_block, n_row_blocks, *, rows, seq):
    ext_ref[0:POOL_HALO, :] = jnp.where(row_block == 0, 0.0, uprev_ref[0])
    ext_ref[POOL_HALO:POOL_HALO + rows, :] = u_ref[0]
    ext_ref[POOL_HALO + rows:, :] = jnp.where(row_block == n_row_blocks - 1, 0.0, unext_ref[0])

    r8 = lax.broadcasted_iota(jnp.int32, (SUBLANES, 1), 0)
    diffs = []
    for g, w in enumerate(POOL_WINDOWS):
        half = w // 2
        sl = slice(g * POOL_CH, (g + 1) * POOL_CH)
        wsum = ext_ref[POOL_HALO - half:POOL_HALO - half + rows, sl]
        for off in range(-half + 1, half):
            wsum = wsum + ext_ref[POOL_HALO + off:POOL_HALO + off + rows, sl]

        def clipped_mean(rows8, first_row):
            t = row_block * rows + first_row + r8
            cnt = (jnp.minimum(t + half, seq) - jnp.maximum(t - half, 0)).astype(F32)
            return rows8 / cnt

        mean = jnp.concatenate([
            clipped_mean(wsum[:SUBLANES], 0),
            wsum[SUBLANES:rows - SUBLANES] * (1.0 / w),
            clipped_mean(wsum[rows - SUBLANES:], rows - SUBLANES)], axis=0)
        diffs.append((mean - ext_ref[POOL_HALO:POOL_HALO + rows, sl]).astype(BF16))
        d_ref[0, :, sl] = diffs[-1]
    return diffs


def _ordered_after(x, deps):
    xw = pltpu.bitcast(x, jnp.uint32)
    n = xw.shape[0]
    for dep in deps:
        word = pltpu.bitcast(dep, jnp.uint32)
        zero = (word >> 16) >> 16
        if zero.shape[0] < n:
            zero = jnp.concatenate([zero] * (n // zero.shape[0]), axis=0)
        for r in range(0, zero.shape[0], n):
            xw = xw | zero[r:r + n]
    return pltpu.bitcast(xw, x.dtype)


def _pool_specs(T, tq):
    rows = tq // N_KV_HEADS
    halo_blocks = rows // POOL_HALO
    n_halo = T // POOL_HALO
    blk = lambda b, g, i: N_KV_HEADS * i + g
    in_specs = [
        pl.BlockSpec((1, rows, POOL_W), lambda b, g, i: (b, blk(b, g, i), 0)),
        pl.BlockSpec((1, POOL_HALO, POOL_W), lambda b, g, i: (b, jnp.maximum(blk(b, g, i) * halo_blocks - 1, 0), 0)),
        pl.BlockSpec((1, POOL_HALO, POOL_W),
                     lambda b, g, i: (b, jnp.minimum((blk(b, g, i) + 1) * halo_blocks, n_halo - 1), 0)),
    ]
    out_spec = pl.BlockSpec((1, rows, POOL_W), lambda b, g, i: (b, blk(b, g, i), 0))
    scratch = pltpu.VMEM((rows + 2 * POOL_HALO, POOL_W), F32)
    return rows, in_specs, out_spec, scratch


def _attn_body(q_ref, qn_ref, k_ref, v_ref, u_ref, uprev_ref, unext_ref, o_ref, d_ref,
               s_ref, mt_ref, m_ref, acc_ref, ext_ref, *, tq, tk, seq):
    n_kt = seq // tk
    lane = lax.broadcasted_iota(jnp.int32, (1, LANES), 1)
    first_head_lane = lane < HEAD_DIM
    contract_last = (((1,), (1,)), ((), ()))

    def scores(slab, qp, j):
        rows = pl.ds(pl.multiple_of(j * tk, tk), tk)
        for head in range(2):
            s = lax.dot_general(qp, k_ref[0, 0, head, rows, :], contract_last, preferred_element_type=F32)
            s_ref[slab, head] = s
            mt_ref[slab, head] = jnp.broadcast_to(jnp.max(s, axis=1, keepdims=True), (tq, LANES))

    def update(slab, j):
        rows = pl.ds(pl.multiple_of(j * tk, tk), tk)
        alphas, pv = [], None
        for head in range(2):
            m_old = m_ref[slab, head]
            m_new = jnp.maximum(m_old, mt_ref[slab, head])
            m_ref[slab, head] = m_new
            alphas.append(jnp.exp2(m_old - m_new))
            p = jnp.concatenate(
                [jnp.exp2(s_ref[slab, head, :, c * LANES:(c + 1) * LANES] - m_new) for c in range(tk // LANES)],
                axis=1).astype(BF16)
            part = jnp.dot(p, v_ref[0, 0, head, rows, :], preferred_element_type=F32)
            pv = part if pv is None else pv + part
        alpha = jnp.where(first_head_lane, alphas[0], alphas[1])
        acc_ref[slab, :, :LANES] = alpha * acc_ref[slab, :, :LANES] + pv[:, :LANES]
        acc_ref[slab, :, LANES:] = alpha * acc_ref[slab, :, LANES:] + pv[:, LANES:]

    pool_rows = tq // N_KV_HEADS
    _pool_diffs(u_ref, uprev_ref, unext_ref, d_ref, ext_ref, N_KV_HEADS * pl.program_id(2) + pl.program_id(1),
                seq // pool_rows, rows=pool_rows, seq=seq)

    acc_ref[...] = jnp.zeros_like(acc_ref)
    m_ref[...] = jnp.full_like(m_ref, -jnp.inf)

    @pl.when(pl.program_id(2) == 0)
    def _first_q_tile():
        scores(0, q_ref[0, :, :LANES], 0)

    def step(j, carry):
        last = j == n_kt - 1
        scores(1, q_ref[0, :, LANES:], j)
        update(0, j)
        scores(0, jnp.where(last, qn_ref[0, :, :LANES], q_ref[0, :, :LANES]), jnp.where(last, 0, j + 1))
        update(1, j)
        return carry

    lax.fori_loop(0, n_kt, step, 0, unroll=2)
    for slab in range(2):
        o_ref[0, :, slab * LANES:(slab + 1) * LANES] = (
            acc_ref[slab, :, :LANES] / acc_ref[slab, :, LANES:]).astype(BF16)


def _attn_call(q, k, v, u):
    B, T, _ = q.shape
    _, tq, tk, _ = _tiles(T)
    n_qt = T // tq
    _, pool_in, pool_out, pool_scratch = _pool_specs(T, tq)
    return pl.pallas_call(
        functools.partial(_attn_body, tq=tq, tk=tk, seq=T),
        grid=(B, N_KV_HEADS, n_qt),
        in_specs=[
            pl.BlockSpec((1, tq, 2 * LANES), lambda b, g, i: (b, i, g)),
            pl.BlockSpec((1, tq, 2 * LANES), lambda b, g, i: (b, jnp.minimum(i + 1, n_qt - 1), g)),
            pl.BlockSpec((1, 1, 2, T, LANES), lambda b, g, i: (b, g, 0, 0, 0)),
            pl.BlockSpec((1, 1, 2, T, 2 * LANES), lambda b, g, i: (b, g, 0, 0, 0)),
            *pool_in,
        ],
        out_specs=[pl.BlockSpec((1, tq, 2 * LANES), lambda b, g, i: (b, i, g)), pool_out],
        out_shape=[jax.ShapeDtypeStruct((B, T, ATTN_W), BF16), jax.ShapeDtypeStruct((B, T, POOL_W), BF16)],
        scratch_shapes=[
            pltpu.VMEM((2, 2, tq, tk), F32),
            pltpu.VMEM((2, 2, tq, LANES), F32),
            pltpu.VMEM((2, 2, tq, LANES), F32),
            pltpu.VMEM((2, tq, 2 * LANES), F32),
            pool_scratch,
        ],
        compiler_params=pltpu.CompilerParams(
            dimension_semantics=("arbitrary", "arbitrary", "arbitrary"), vmem_limit_bytes=VMEM_LIMIT_BYTES),
        name="attn",
    )(q, q, k, v, u, u, u)


def _attn_bounded_body(q_ref, k_ref, v_ref, u_ref, uprev_ref, unext_ref, o_ref, d_ref, acc_ref, ext_ref,
                       *, tq, tk, tiles_per_trip, seq):
    contract_last = (((1,), (1,)), ((), ()))
    pool_rows = tq // N_KV_HEADS
    diffs = _pool_diffs(u_ref, uprev_ref, unext_ref, d_ref, ext_ref,
                        N_KV_HEADS * pl.program_id(2) + pl.program_id(1), seq // pool_rows, rows=pool_rows, seq=seq)
    acc_ref[...] = jnp.zeros_like(acc_ref)

    pieces = []
    for diff, w in zip(diffs, POOL_WINDOWS):
        step = pool_rows // max(1, w // 2)
        pieces += [[diff[r:r + step]] for r in range(0, pool_rows, step)]
    anchor_every = max(1, tiles_per_trip // (len(pieces) + 1))

    def trip(jj, carry):
        for slab in range(2):
            qp = q_ref[0, :, slab * LANES:(slab + 1) * LANES]
            total = None
            for t in range(tiles_per_trip):
                anchored = slab == 0 and t % anchor_every == 0 and 0 < t // anchor_every <= len(pieces)
                rows = pl.ds(pl.multiple_of((jj * tiles_per_trip + t) * tk, tk), tk)
                for head in range(2):
                    s = lax.dot_general(qp, k_ref[0, 0, head, rows, :], contract_last, preferred_element_type=F32)
                    vt = v_ref[0, 0, head, rows, :]
                    if anchored and head == 0:
                        ones = _ordered_after(vt[:, LANES:], pieces[t // anchor_every - 1])
                        vt = jnp.concatenate([vt[:, :LANES], ones], axis=1)
                    term = jnp.dot(jnp.exp2(s).astype(BF16), vt, preferred_element_type=F32)
                    total = term if total is None else total + term
            acc_ref[slab] += total
        return carry

    lax.fori_loop(0, seq // (tk * tiles_per_trip), trip, 0)
    for slab in range(2):
        o_ref[0, :, slab * LANES:(slab + 1) * LANES] = (
            acc_ref[slab, :, :LANES] / acc_ref[slab, :, LANES:]).astype(BF16)


def _attn_bounded_call(q, k, v, u):
    B, T, _ = q.shape
    tq = min(T, max(BOUNDED_Q_TILE, BOUNDED_CHAIN_ROWS // T))
    tk = min(BOUNDED_KEY_TILE, T)
    _, pool_in, pool_out, pool_scratch = _pool_specs(T, tq)
    return pl.pallas_call(
        functools.partial(_attn_bounded_body, tq=tq, tk=tk, tiles_per_trip=min(BOUNDED_TILES_PER_TRIP, T // tk), seq=T),
        grid=(B, N_KV_HEADS, T // tq),
        in_specs=[
            pl.BlockSpec((1, tq, 2 * LANES), lambda b, g, i: (b, i, g)),
            pl.BlockSpec((1, 1, 2, T, LANES), lambda b, g, i: (b, g, 0, 0, 0)),
            pl.BlockSpec((1, 1, 2, T, 2 * LANES), lambda b, g, i: (b, g, 0, 0, 0)),
            *pool_in,
        ],
        out_specs=[pl.BlockSpec((1, tq, 2 * LANES), lambda b, g, i: (b, i, g)), pool_out],
        out_shape=[jax.ShapeDtypeStruct((B, T, ATTN_W), BF16), jax.ShapeDtypeStruct((B, T, POOL_W), BF16)],
        scratch_shapes=[pltpu.VMEM((2, tq, 2 * LANES), F32), pool_scratch],
        compiler_params=pltpu.CompilerParams(
            dimension_semantics=("arbitrary", "arbitrary", "arbitrary"), vmem_limit_bytes=VMEM_LIMIT_BYTES),
        name="attn_bounded",
    )(q, k, v, u, u, u)


def _mix_body(x_ref, o_ref, d_ref, p_ref,
              wpool_ref, pscale_ref, wout_ref, gmlp_ref, wup_ref, wdown_ref,
              gple_ref, wgate_ref, wproj_ref, gfin_ref,
              y_ref, *, ff_chunk, final_norm):
    pooled = []
    for g in range(len(POOL_WINDOWS)):
        sl = slice(g * POOL_CH, (g + 1) * POOL_CH)
        yg = jnp.dot(d_ref[0, :, sl], wpool_ref[g], preferred_element_type=F32) * pscale_ref[:, sl]
        pooled.append(yg.astype(BF16))
    mixed = jnp.concatenate([o_ref[0]] + pooled, axis=1)
    h = x_ref[0] + jnp.dot(mixed, wout_ref[...], preferred_element_type=F32)

    m = (h * gmlp_ref[...]).astype(BF16)
    r_mlp = _row_rsqrt(h)
    d_ff = wup_ref.shape[1]
    mlp = None
    for c in range(d_ff // ff_chunk):
        cs = slice(c * ff_chunk, (c + 1) * ff_chunk)
        up = jnp.dot(m, wup_ref[:, cs], preferred_element_type=F32) * r_mlp
        act = jnp.square(jnp.maximum(up, 0.0)).astype(BF16)
        part = jnp.dot(act, wdown_ref[cs, :], preferred_element_type=F32)
        mlp = part if mlp is None else mlp + part
    h = h + mlp

    gate_in = (h * gple_ref[...]).astype(BF16)
    gate = jax.nn.sigmoid(jnp.dot(gate_in, wgate_ref[...], preferred_element_type=F32) * _row_rsqrt(h))
    emb = jnp.dot(p_ref[0].astype(BF16), wproj_ref[...], preferred_element_type=F32)
    h = h + gate * emb

    y_ref[0] = _rms(h, gfin_ref[...]) if final_norm else h


def _mix_call(x, o, d, p, wpool, pscale, wout, gmlp, wup, wdown, gple, wgate, wproj, gfin, final_norm):
    B, T, D = x.shape
    tm = _tiles(T)[3]
    d_ff = wup.shape[1]
    ple = p.shape[-1]

    def const(ndim):
        return lambda b, i: (0,) * ndim

    def resident(shape):
        return pl.BlockSpec(shape, const(len(shape)), pipeline_mode=pl.Buffered(1))

    tile = lambda w: pl.BlockSpec((1, tm, w), lambda b, i: (b, i, 0))
    return pl.pallas_call(
        functools.partial(_mix_body, ff_chunk=min(2048, d_ff), final_norm=final_norm),
        grid=(B, T // tm),
        in_specs=[
            tile(D),
            tile(ATTN_W),
            tile(POOL_W),
            tile(ple),
            resident((len(POOL_WINDOWS), POOL_CH, POOL_CH)),
            resident((1, POOL_W)),
            resident((ATTN_W + POOL_W, D)),
            resident((1, D)),
            resident((D, d_ff)),
            resident((d_ff, D)),
            resident((1, D)),
            resident((D, D)),
            resident((ple, D)),
            resident((1, D)),
        ],
        out_specs=tile(D),
        out_shape=jax.ShapeDtypeStruct((B, T, D), F32),
        compiler_params=pltpu.CompilerParams(
            dimension_semantics=("arbitrary", "arbitrary"), vmem_limit_bytes=VMEM_LIMIT_BYTES),
        name="mix",
    )(x, o, d, p, wpool, pscale, wout, gmlp, wup, wdown, gple, wgate, wproj, gfin)


def _scores_bounded(q_gain, k_gain):
    bound = math.sqrt(HEAD_DIM) * math.log2(math.e) * jnp.max(jnp.abs(q_gain)) * jnp.max(jnp.abs(k_gain))
    return bound * 1.01 <= SCORE_BOUND_LOG2


def _trunk(x, p, consts, layers, final_g):
    invf, bd = consts
    h = x
    for li, lw in enumerate(layers):
        q, k, v, u = _proj_call(h, lw["gmix"], lw["w_in"], lw["gq"], lw["gk"], invf, bd)
        o, d = lax.cond(lw["scores_bounded"], _attn_bounded_call, _attn_call, q, k, v, u)
        h = _mix_call(h, o, d, p[li], lw["wpool"], lw["pscale"], lw["wout"], lw["gmlp"], lw["wup"], lw["wdown"],
                      lw["gple"], lw["wgate"], lw["wproj"], final_g, final_norm=(li == len(layers) - 1))
    return h


def kernel(x_prompt, x_sample, p_prompt, p_sample, norm_mix_g, w_in, q_norm_g, k_norm_g, w_pool, pool_scale, w_out, norm_mlp_g, w_up, w_down, norm_ple_g, w_ple_gate, w_ple_proj, final_norm_g):
    depth = w_in.shape[0]
    row = lambda g: g.reshape(1, -1).astype(F32)
    layers = []
    for li in range(depth):
        layers.append(dict(
            scores_bounded=_scores_bounded(q_norm_g[li], k_norm_g[li]),
            gmix=row(norm_mix_g[li]), w_in=w_in[li].astype(BF16),
            gq=row(jnp.tile(q_norm_g[li], N_Q_HEADS)), gk=row(jnp.tile(k_norm_g[li], N_KV_HEADS)),
            wpool=w_pool[li].astype(BF16), pscale=row(pool_scale[li]), wout=w_out[li].astype(BF16),
            gmlp=row(norm_mlp_g[li]), wup=w_up[li].astype(BF16), wdown=w_down[li].astype(BF16),
            gple=row(norm_ple_g[li]), wgate=w_ple_gate[li].astype(BF16), wproj=w_ple_proj[li].astype(BF16)))
    inv_freq = ROPE_THETA ** (-jnp.arange(ROPE_PAIRS, dtype=F32) / ROPE_PAIRS)
    invf = jnp.tile(inv_freq, LANES // ROPE_PAIRS).reshape(1, LANES)
    head_of = jnp.arange(ATTN_W) // HEAD_DIM
    bd = (head_of[:, None] == head_of[None, :]).astype(BF16)
    consts = (invf, bd)
    final_g = row(final_norm_g)
    y_prompt = _trunk(x_prompt, p_prompt, consts, layers, final_g)
    y_sample = _trunk(x_sample, p_sample, consts, layers, final_g)
    return (y_prompt, y_sample)
```
